```python
import math
import jax, jax.numpy as jnp
from jax import lax
import numpy as np

D_MODEL = 1024
BATCH = 1
SEQ = 16384
DEPTH = 2
DEC_BATCH = 32
DEC_SEQ = 8
PAST_LEN = 16384
PAGE_SIZE = 128

N_META = 16
N_A_LAYERS = DEPTH // 2
N_B_LAYERS = DEPTH - N_A_LAYERS
CONV_WIDTH = 31
D_CONV = D_MODEL
N_HEADS = 8
HEAD_DIM = 64
V_DIM = 2 * HEAD_DIM
D_QK = N_HEADS * 2 * HEAD_DIM
D_V = N_HEADS * V_DIM
D_FF = ((8 * D_MODEL // 3 + 127) // 128) * 128
FFN_CONV_WIDTH = 3
Q_BLOCK = 128
EPS = 1e-6
NEG = -1e30

kernel_name = 'yoco_conformer_diffattn_decoder_step'


def alibi_slopes():
    return jnp.exp2(-8.0 * jnp.arange(1, N_HEADS + 1, dtype=jnp.float32) / N_HEADS)


def rms_norm(x, g):
    x32 = x.astype(jnp.float32)
    y = x32 * lax.rsqrt(jnp.mean(x32 * x32, axis=-1, keepdims=True) + EPS)
    return (y * g.astype(jnp.float32)).astype(x.dtype)


def layer_norm(x, g, b):
    x32 = x.astype(jnp.float32)
    mu = jnp.mean(x32, axis=-1, keepdims=True)
    xc = x32 - mu
    y = xc * lax.rsqrt(jnp.mean(xc * xc, axis=-1, keepdims=True) + EPS)
    return (y * g.astype(jnp.float32) + b.astype(jnp.float32)).astype(x.dtype)


def causal_dwconv(x, past, w, b):
    xp = jnp.concatenate([past.astype(x.dtype), x], axis=1)
    y = lax.conv_general_dilated(xp, w[:, None, :].astype(x.dtype), (1,), 'VALID',
                                 dimension_numbers=('NWC', 'WIO', 'NWC'),
                                 feature_group_count=x.shape[-1])
    return y + b.astype(x.dtype), xp[:, xp.shape[1] - (w.shape[0] - 1):]


def conformer_conv(u, past, w_pw1, w_dw, b_dw, ln_g, ln_b, w_pw2):
    pg = u @ w_pw1
    glu = pg[..., :D_CONV] * jax.nn.sigmoid(pg[..., D_CONV:])
    c, new_past = causal_dwconv(glu, past, w_dw, b_dw)
    c = layer_norm(c, ln_g, ln_b)
    return jax.nn.silu(c) @ w_pw2, new_past


def conv_ffn(u, past, w_in, w_dw, b_dw, w_down):
    ag = u @ w_in
    a, new_past = causal_dwconv(ag[..., :D_FF], past, w_dw, b_dw)
    return (jax.nn.silu(a) * ag[..., D_FF:]) @ w_down, new_past


def shared_kv(h, g_kv, w_kv, g_k):
    n, length = h.shape[:2]
    kv = rms_norm(h, g_kv) @ w_kv
    k = rms_norm(kv[..., :D_QK].reshape(n, length, N_HEADS, 2, HEAD_DIM), g_k)
    v = kv[..., D_QK:].reshape(n, length, N_HEADS, V_DIM)
    return k, v


def diff_weights(s, lam):
    p = jax.nn.softmax(s, axis=-1)
    return p[:, :, 0] - lam * p[:, :, 1]


def prompt_attention(q, k, v, lam, slopes):
    n, t = q.shape[:2]
    n_blk = -(-t // Q_BLOCK)
    tp = n_blk * Q_BLOCK
    pad = ((0, 0), (0, tp - t), (0, 0), (0, 0), (0, 0))
    qp, kp, vp = jnp.pad(q, pad), jnp.pad(k, pad), jnp.pad(v, pad[:4])
    q_blocks = qp.reshape(n, n_blk, Q_BLOCK, N_HEADS, 2, HEAD_DIM).transpose(1, 0, 2, 3, 4, 5)
    key_pos = jnp.arange(tp)
    scale = HEAD_DIM ** -0.5
    slope_b = slopes[:, None, None, None]

    def block(args):
        qb, start = args
        s = jnp.einsum('nqhcd,nkhcd->nhcqk', qb, kp, preferred_element_type=jnp.float32) * scale
        dist = ((start + jnp.arange(Q_BLOCK))[:, None] - key_pos[None, :]).astype(jnp.float32)
        s = jnp.where(dist >= 0, s - slope_b * dist, NEG)
        w = diff_weights(s, lam).astype(vp.dtype)
        return jnp.einsum('nhqk,nkhe->nqhe', w, vp)

    out = lax.map(block, (q_blocks, jnp.arange(n_blk, dtype=jnp.int32) * Q_BLOCK))
    return out.transpose(1, 0, 2, 3, 4).reshape(n, tp, N_HEADS, V_DIM)[:, :t]


def sample_attention(q, k_new, v_new, k_past, v_past, lam, slopes):
    p_len, nq = k_past.shape[1], q.shape[1]
    scale = HEAD_DIM ** -0.5
    slope_b = slopes[:, None, None, None]
    s_past = jnp.einsum('nqhcd,nkhcd->nhcqk', q, k_past, preferred_element_type=jnp.float32) * scale
    s_new = jnp.einsum('nqhcd,nkhcd->nhcqk', q, k_new, preferred_element_type=jnp.float32) * scale
    qi = jnp.arange(nq)
    dist_past = ((p_len + qi)[:, None] - jnp.arange(p_len)[None, :]).astype(jnp.float32)
    dist_new = (qi[:, None] - qi[None, :]).astype(jnp.float32)
    s_past = s_past - slope_b * dist_past
    s_new = jnp.where(dist_new >= 0, s_new - slope_b * dist_new, NEG)
    w = diff_weights(jnp.concatenate([s_past, s_new], axis=-1), lam).astype(v_new.dtype)
    return (jnp.einsum('nhqk,nkhe->nqhe', w[..., :p_len], v_past)
            + jnp.einsum('nhqk,nkhe->nqhe', w[..., p_len:], v_new))


def trunk(h, conv_past, ffn_past, attend, g_mix, g_ffn, w_pw1, w_dw, b_dw, ln_g, ln_b, w_pw2,
          g_kv, w_kv, g_k, w_q, g_q, lambda_params, g_subln, w_o,
          w_ffn_in, w_ffn_dw, b_ffn_dw, w_ffn_down):
    n, length = h.shape[:2]
    new_conv, new_ffn = [], []
    k = v = None
    for l in range(DEPTH):
        u = rms_norm(h, g_mix[l])
        if l < N_A_LAYERS:
            y, st = conformer_conv(u, conv_past[l], w_pw1[l], w_dw[l], b_dw[l],
                                   ln_g[l], ln_b[l], w_pw2[l])
            new_conv.append(st)
        else:
            if l == N_A_LAYERS:
                k, v = shared_kv(h, g_kv, w_kv, g_k)
            b = l - N_A_LAYERS
            lam_init = 0.8 - 0.6 * math.exp(-0.3 * l)
            q = rms_norm((u @ w_q[b]).reshape(n, length, N_HEADS, 2, HEAD_DIM), g_q[b])
            lp = lambda_params[b].astype(jnp.float32)
            lam = jnp.exp(jnp.sum(lp[0] * lp[1])) - jnp.exp(jnp.sum(lp[2] * lp[3])) + lam_init
            o = attend(q, k, v, lam)
            o = rms_norm(o, g_subln[b]) * (1.0 - lam_init)
            y = o.reshape(n, length, D_V) @ w_o[b]
        h = h + y
        f, st = conv_ffn(rms_norm(h, g_ffn[l]), ffn_past[l], w_ffn_in[l], w_ffn_dw[l],
                         b_ffn_dw[l], w_ffn_down[l])
        new_ffn.append(st)
        h = h + f
    return h, k, v, jnp.stack(new_conv), jnp.stack(new_ffn)


def setup_inputs(seed: int = 0) -> dict:
    key = jax.random.key(seed)
    ks = jax.random.split(key, 32)
    f32 = jnp.float32
    n_pages = PAST_LEN // PAGE_SIZE
    n_used = DEC_BATCH * n_pages
    n_pool = n_used + max(1, n_used // 4)

    def nrm(k, shape, scale):
        return jax.random.normal(k, shape, f32) * scale

    def gain(k, shape):
        return 1.0 + 0.02 * jax.random.normal(k, shape, f32)

    page_table = jax.random.permutation(ks[0], n_pool)[:n_used].reshape(DEC_BATCH, n_pages).astype(jnp.int32)
    return {
        'x_prompt': nrm(ks[1], (BATCH, SEQ, D_MODEL), 1.0),
        'x_sample': nrm(ks[2], (DEC_BATCH, DEC_SEQ, D_MODEL), 1.0),
        'cache_k': nrm(ks[3], (n_pool, PAGE_SIZE, N_HEADS, 2 * HEAD_DIM), 1.0),
        'cache_v': nrm(ks[4], (n_pool, PAGE_SIZE, N_HEADS, V_DIM), 1.0),
        'state_conv': nrm(ks[5], (N_A_LAYERS, DEC_BATCH, CONV_WIDTH - 1, D_CONV), 1.0),
        'state_ffn': nrm(ks[6], (DEPTH, DEC_BATCH, FFN_CONV_WIDTH - 1, D_FF), 1.0),
        'page_table': page_table,
        'meta_tokens': nrm(ks[7], (N_META, D_MODEL), 1.0),
        'g_mix': gain(ks[8], (DEPTH, D_MODEL)),
        'g_ffn': gain(ks[9], (DEPTH, D_MODEL)),
        'w_pw1': nrm(ks[10], (N_A_LAYERS, D_MODEL, 2 * D_CONV), D_MODEL ** -0.5),
        'w_dw': nrm(ks[11], (N_A_LAYERS, CONV_WIDTH, D_CONV), CONV_WIDTH ** -0.5),
        'b_dw': nrm(ks[12], (N_A_LAYERS, D_CONV), 0.01),
        'ln_g': gain(ks[13], (N_A_LAYERS, D_CONV)),
        'ln_b': nrm(ks[14], (N_A_LAYERS, D_CONV), 0.01),
        'w_pw2': nrm(ks[15], (N_A_LAYERS, D_CONV, D_MODEL), D_CONV ** -0.5),
        'g_kv': gain(ks[16], (D_MODEL,)),
        'w_kv': nrm(ks[17], (D_MODEL, D_QK + D_V), D_MODEL ** -0.5),
        'g_k': gain(ks[18], (2, HEAD_DIM)),
        'w_q': nrm(ks[19], (N_B_LAYERS, D_MODEL, D_QK), D_MODEL ** -0.5),
        'g_q': gain(ks[20], (N_B_LAYERS, 2, HEAD_DIM)),
        'lambda_params': nrm(ks[21], (N_B_LAYERS, 4, HEAD_DIM), 0.1),
        'g_subln': gain(ks[22], (N_B_LAYERS, V_DIM)),
        'w_o': nrm(ks[23], (N_B_LAYERS, D_V, D_MODEL), D_V ** -0.5),
        'w_ffn_in': nrm(ks[24], (DEPTH, D_MODEL, 2 * D_FF), D_MODEL ** -0.5),
        'w_ffn_dw': nrm(ks[25], (DEPTH, FFN_CONV_WIDTH, D_FF), FFN_CONV_WIDTH ** -0.5),
        'b_ffn_dw': nrm(ks[26], (DEPTH, D_FF), 0.01),
        'w_ffn_down': nrm(ks[27], (DEPTH, D_FF, D_MODEL), D_FF ** -0.5),
    }


def reference(x_prompt, x_sample, cache_k, cache_v, state_conv, state_ffn, page_table,
              meta_tokens, g_mix, g_ffn, w_pw1, w_dw, b_dw, ln_g, ln_b, w_pw2,
              g_kv, w_kv, g_k, w_q, g_q, lambda_params, g_subln, w_o,
              w_ffn_in, w_ffn_dw, b_ffn_dw, w_ffn_down):
    weights = (g_mix, g_ffn, w_pw1, w_dw, b_dw, ln_g, ln_b, w_pw2, g_kv, w_kv, g_k,
               w_q, g_q, lambda_params, g_subln, w_o, w_ffn_in, w_ffn_dw, b_ffn_dw, w_ffn_down)
    slopes = alibi_slopes()

    n_p = x_prompt.shape[0]
    meta = jnp.broadcast_to(meta_tokens.astype(x_prompt.dtype)[None], (n_p, N_META, D_MODEL))
    h_p = jnp.concatenate([meta, x_prompt], axis=1)
    conv0 = jnp.zeros((N_A_LAYERS, n_p, CONV_WIDTH - 1, D_CONV), x_prompt.dtype)
    ffn0 = jnp.zeros((DEPTH, n_p, FFN_CONV_WIDTH - 1, D_FF), x_prompt.dtype)
    prompt_attend = lambda q, k, v, lam: prompt_attention(q, k, v, lam, slopes)
    h_p, k_p, v_p, conv_p, ffn_p = trunk(h_p, conv0, ffn0, prompt_attend, *weights)

    n_s = x_sample.shape[0]
    past_len = page_table.shape[1] * PAGE_SIZE
    k_past = cache_k[page_table].reshape(n_s, past_len, N_HEADS, 2, HEAD_DIM)
    v_past = cache_v[page_table].reshape(n_s, past_len, N_HEADS, V_DIM)
    sample_attend = lambda q, k, v, lam: sample_attention(q, k, v, k_past, v_past, lam, slopes)
    h_s, k_s, v_s, conv_s, ffn_s = trunk(x_sample, state_conv, state_ffn, sample_attend, *weights)

    y_prompt = h_p[:, N_META:]
    k_prompt = k_p.reshape(n_p, k_p.shape[1], N_HEADS, 2 * HEAD_DIM)
    k_sample = k_s.reshape(n_s, k_s.shape[1], N_HEADS, 2 * HEAD_DIM)
    return (y_prompt, h_s, k_prompt, v_p, k_sample, v_s, conv_p, conv_s, ffn_p, ffn_s)
```

```python
import functools
import math

import numpy as np
import jax
import jax.numpy as jnp
from jax import lax
from jax.experimental import pallas as pl
from jax.experimental.pallas import tpu as pltpu

F32 = jnp.float32
BF16 = jnp.bfloat16

D_MODEL = 1024
N_META = 16
CONV_WIDTH = 31
D_CONV = D_MODEL
N_HEADS = 8
HEAD_DIM = 64
V_DIM = 2 * HEAD_DIM
D_QK = N_HEADS * 2 * HEAD_DIM
D_V = N_HEADS * V_DIM
D_FF = ((8 * D_MODEL // 3 + 127) // 128) * 128
FFN_CONV_WIDTH = 3
PAGE_SIZE = 128
EPS = 1e-6
NEG = -1e30
LAMBDA_INIT_L1 = 0.8 - 0.6 * math.exp(-0.3 * 1)

LANES = 128
SUBLANES = 8
MXU_DIM = 256
VMEM_LIMIT_BYTES = 56 * 1024 * 1024

CONV_HALO = CONV_WIDTH - 1
CONV_HALO_PAD = 32
FFN_HALO = FFN_CONV_WIDTH - 1
FFN_HALO_PAD = SUBLANES

ROW_TILE = 512
FFN_ROW_TILE = 256
ATTN_TQ = 512
ATTN_TK = 1024
PAGES_PER_STEP = 8


def _const_spec(shape):
    zeros = (0,) * len(shape)
    return pl.BlockSpec(shape, lambda *_: zeros, pipeline_mode=pl.Buffered(1))


def _params(n_axes):
    return pltpu.CompilerParams(
        dimension_semantics=("arbitrary",) * n_axes,
        vmem_limit_bytes=VMEM_LIMIT_BYTES,
    )


def _rms_norm(x, g):
    return x * lax.rsqrt(jnp.mean(x * x, axis=-1, keepdims=True) + EPS) * g


def _silu(x):
    return x * jax.nn.sigmoid(x)


def _conv_module_kernel(h_ref, past_ref, g_ref, wpw1_ref, wdw_ref, bdw_ref, lng_ref, lnb_ref, wpw2_ref,
                        out_ref, state_ref, buf_ref, *, nb, tm):
    lo = CONV_HALO_PAD - CONV_HALO

    @pl.when(pl.program_id(1) == 0)
    def _load_history():
        buf_ref[:, lo:CONV_HALO_PAD, :] = past_ref[...]

    x = h_ref[...].reshape(nb * tm, D_MODEL)
    u = _rms_norm(x, g_ref[...]).astype(BF16)
    pg = jnp.dot(u, wpw1_ref[...], preferred_element_type=F32)
    glu = pg[:, :D_CONV] * jax.nn.sigmoid(pg[:, D_CONV:])
    buf_ref[:, CONV_HALO_PAD:CONV_HALO_PAD + tm, :] = glu.reshape(nb, tm, D_CONV)

    acc = jnp.broadcast_to(bdw_ref[...].reshape(1, 1, D_CONV), (nb, tm, D_CONV))
    for w in range(CONV_WIDTH):
        acc = acc + wdw_ref[w:w + 1, :].reshape(1, 1, D_CONV) * buf_ref[:, lo + w:lo + w + tm, :]

    new_state = buf_ref[:, tm + lo:tm + CONV_HALO_PAD, :]
    state_ref[...] = new_state
    buf_ref[:, lo:CONV_HALO_PAD, :] = new_state

    c = acc.reshape(nb * tm, D_CONV)
    mu = jnp.mean(c, axis=-1, keepdims=True)
    cc = c - mu
    c = cc * lax.rsqrt(jnp.mean(cc * cc, axis=-1, keepdims=True) + EPS) * lng_ref[...] + lnb_ref[...]
    y = jnp.dot(_silu(c).astype(BF16), wpw2_ref[...], preferred_element_type=F32)
    out_ref[...] = (x + y).reshape(nb, tm, D_MODEL)


def _conv_module(h, past, g, wpw1, wdw, bdw, lng, lnb, wpw2, *, nb, tm):
    ns, length, _ = h.shape
    grid = (ns // nb, length // tm)
    kern = functools.partial(_conv_module_kernel, nb=nb, tm=tm)
    return pl.pallas_call(
        kern,
        grid=grid,
        in_specs=[
            pl.BlockSpec((nb, tm, D_MODEL), lambda s, t: (s, t, 0)),
            pl.BlockSpec((nb, CONV_HALO, D_CONV), lambda s, t: (s, 0, 0)),
            _const_spec((1, D_MODEL)),
            _const_spec((D_MODEL, 2 * D_CONV)),
            _const_spec((CONV_WIDTH, D_CONV)),
            _const_spec((1, D_CONV)),
            _const_spec((1, D_CONV)),
            _const_spec((1, D_CONV)),
            _const_spec((D_CONV, D_MODEL)),
        ],
        out_specs=[
            pl.BlockSpec((nb, tm, D_MODEL), lambda s, t: (s, t, 0)),
            pl.BlockSpec((nb, CONV_HALO, D_CONV), lambda s, t: (s, 0, 0)),
        ],
        out_shape=[
            jax.ShapeDtypeStruct((ns, length, D_MODEL), F32),
            jax.ShapeDtypeStruct((ns, CONV_HALO, D_CONV), F32),
        ],
        scratch_shapes=[pltpu.VMEM((nb, CONV_HALO_PAD + tm, D_CONV), F32)],
        compiler_params=_params(2),
        name="conv_module",
    )(h, past, g, wpw1, wdw, bdw, lng, lnb, wpw2)


def _ffn_kernel(*refs, nb, tm, with_proj):
    if with_proj:
        (h_ref, o_ref, wo_ref, past_ref, g_ref, win_ref, wdw_ref, bdw_ref, wdown_ref,
         out_ref, state_ref, buf_ref) = refs
    else:
        (h_ref, past_ref, g_ref, win_ref, wdw_ref, bdw_ref, wdown_ref,
         out_ref, state_ref, buf_ref) = refs
    lo = FFN_HALO_PAD - FFN_HALO

    @pl.when(pl.program_id(1) == 0)
    def _load_history():
        buf_ref[:, lo:FFN_HALO_PAD, :] = past_ref[...]

    x = h_ref[...].reshape(nb * tm, D_MODEL)
    if with_proj:
        o = o_ref[...].reshape(nb * tm, D_V).astype(BF16)
        x = x + jnp.dot(o, wo_ref[...], preferred_element_type=F32)
    u = _rms_norm(x, g_ref[...]).astype(BF16)
    ag = jnp.dot(u, win_ref[...], preferred_element_type=F32)
    a = ag[:, :D_FF].reshape(nb, tm, D_FF)
    gate = ag[:, D_FF:]
    buf_ref[:, FFN_HALO_PAD:FFN_HALO_PAD + tm, :] = a

    conv = (bdw_ref[...].reshape(1, 1, D_FF)
            + wdw_ref[0:1, :].reshape(1, 1, D_FF) * buf_ref[:, lo:lo + tm, :]
            + wdw_ref[1:2, :].reshape(1, 1, D_FF) * buf_ref[:, lo + 1:lo + 1 + tm, :]
            + wdw_ref[2:3, :].reshape(1, 1, D_FF) * a)

    new_state = buf_ref[:, tm + lo:tm + FFN_HALO_PAD, :]
    state_ref[...] = new_state
    buf_ref[:, lo:FFN_HALO_PAD, :] = new_state

    act = (_silu(conv.reshape(nb * tm, D_FF)) * gate).astype(BF16)
    f = jnp.dot(act, wdown_ref[...], preferred_element_type=F32)
    out_ref[...] = (x + f).reshape(nb, tm, D_MODEL)


def _ffn(h, past, g, win, wdw, bdw, wdown, *, nb, tm, o=None, wo=None):
    ns, length, _ = h.shape
    grid = (ns // nb, length // tm)
    with_proj = o is not None
    kern = functools.partial(_ffn_kernel, nb=nb, tm=tm, with_proj=with_proj)
    row_spec = pl.BlockSpec((nb, tm, D_MODEL), lambda s, t: (s, t, 0))
    state_spec = pl.BlockSpec((nb, FFN_HALO, D_FF), lambda s, t: (s, 0, 0))
    in_specs = [row_spec]
    args = [h]
    if with_proj:
        in_specs += [pl.BlockSpec((nb, tm, D_V), lambda s, t: (s, t, 0)), _const_spec((D_V, D_MODEL))]
        args += [o, wo]
    in_specs += [
        state_spec,
        _const_spec((1, D_MODEL)),
        _const_spec((D_MODEL, 2 * D_FF)),
        _const_spec((FFN_CONV_WIDTH, D_FF)),
        _const_spec((1, D_FF)),
        _const_spec((D_FF, D_MODEL)),
    ]
    args += [past, g, win, wdw, bdw, wdown]
    return pl.pallas_call(
        kern,
        grid=grid,
        in_specs=in_specs,
        out_specs=[row_spec, state_spec],
        out_shape=[
            jax.ShapeDtypeStruct((ns, length, D_MODEL), F32),
            jax.ShapeDtypeStruct((ns, FFN_HALO, D_FF), F32),
        ],
        scratch_shapes=[pltpu.VMEM((nb, FFN_HALO_PAD + tm, D_FF), F32)],
        compiler_params=_params(2),
        name="conv_ffn_proj" if with_proj else "conv_ffn",
    )(*args)


def _group_rms_norm(x, gain, ones_blk):
    x2 = x * x
    hi = x2.astype(BF16)
    lo = (x2 - hi.astype(F32)).astype(BF16)
    parts = []
    for j in range(x.shape[-1] // MXU_DIM):
        sl = slice(j * MXU_DIM, (j + 1) * MXU_DIM)
        parts.append(jnp.dot(hi[:, sl], ones_blk, preferred_element_type=F32)
                     + jnp.dot(lo[:, sl], ones_blk, preferred_element_type=F32))
    ss = jnp.concatenate(parts, axis=1)
    return x * lax.rsqrt(ss * (1.0 / HEAD_DIM) + EPS) * gain


def _kvq_kernel(h_ref, gkv_ref, wkv_ref, gk_ref, gmix_ref, wq_ref, gq_ref, ones_ref,
                k_ref, v_ref, kh_ref, vh_ref, qh_ref):
    x = h_ref[...]
    ones_blk = ones_ref[...]
    kv = jnp.dot(_rms_norm(x, gkv_ref[...]).astype(BF16), wkv_ref[...], preferred_element_type=F32)
    k = _group_rms_norm(kv[:, :D_QK], gk_ref[...], ones_blk)
    v = kv[:, D_QK:]
    q = jnp.dot(_rms_norm(x, gmix_ref[...]).astype(BF16), wq_ref[...], preferred_element_type=F32)
    q = _group_rms_norm(q, gq_ref[...], ones_blk) * (HEAD_DIM ** -0.5)
    k_ref[...] = k
    v_ref[...] = v
    kb, vb, qb = k.astype(BF16), v.astype(BF16), q.astype(BF16)
    for h in range(N_HEADS):
        sl = slice(h * V_DIM, (h + 1) * V_DIM)
        kh_ref[h] = kb[:, sl]
        vh_ref[h] = vb[:, sl]
        qh_ref[h] = qb[:, sl]


def _kvq(h2d, gkv, wkv, gk_t, gmix, wq, gq_t, ones_blk, *, tm):
    rows = h2d.shape[0]
    row_spec = pl.BlockSpec((tm, D_MODEL), lambda t: (t, 0))
    head_spec = pl.BlockSpec((N_HEADS, tm, V_DIM), lambda t: (0, t, 0))
    return pl.pallas_call(
        _kvq_kernel,
        grid=(rows // tm,),
        in_specs=[
            row_spec,
            _const_spec((1, D_MODEL)),
            _const_spec((D_MODEL, D_QK + D_V)),
            _const_spec((1, D_QK)),
            _const_spec((1, D_MODEL)),
            _const_spec((D_MODEL, D_QK)),
            _const_spec((1, D_QK)),
            _const_spec((MXU_DIM, MXU_DIM)),
        ],
        out_specs=[row_spec, row_spec, head_spec, head_spec, head_spec],
        out_shape=[
            jax.ShapeDtypeStruct((rows, D_QK), F32),
            jax.ShapeDtypeStruct((rows, D_V), F32),
            jax.ShapeDtypeStruct((N_HEADS, rows, V_DIM), BF16),
            jax.ShapeDtypeStruct((N_HEADS, rows, V_DIM), BF16),
            jax.ShapeDtypeStruct((N_HEADS, rows, V_DIM), BF16),
        ],
        compiler_params=_params(1),
        name="kvq_proj",
    )(h2d, gkv, wkv, gk_t, gmix, wq, gq_t, ones_blk)


def _lambda_value(lam_ref):
    lp = lam_ref[...]
    a = jnp.sum(lp[0:1, :] * lp[1:2, :], axis=-1, keepdims=True)
    b = jnp.sum(lp[2:3, :] * lp[3:4, :], axis=-1, keepdims=True)
    return jnp.exp(a) - jnp.exp(b) + LAMBDA_INIT_L1


def _sub_layer_norm(o, gsub):
    o = o * lax.rsqrt(jnp.mean(o * o, axis=-1, keepdims=True) + EPS) * gsub
    return o * (1.0 - LAMBDA_INIT_L1)


def _online_softmax_update(s, v, m_ref, l_ref, acc_ref, idx):
    m_old = m_ref[idx]
    m_new = jnp.maximum(m_old, jnp.max(s, axis=-1, keepdims=True))
    alpha = jnp.exp(m_old - m_new)
    p = jnp.exp(s - m_new)
    l_ref[idx] = alpha * l_ref[idx] + jnp.sum(p, axis=-1, keepdims=True)
    acc_ref[idx] = alpha * acc_ref[idx] + jnp.dot(p.astype(BF16), v, preferred_element_type=F32)
    m_ref[idx] = m_new


def _prompt_attn_kernel(*refs, tq, tk, has_main, qpos_base):
    if has_main:
        (qi_tab, ki_tab, last_tab, slopes_ref, q_ref, k_ref, v_ref, km_ref, vm_ref, lam_ref, gsub_ref,
         o_ref, m_ref, l_ref, acc_ref) = refs
    else:
        (qi_tab, ki_tab, last_tab, slopes_ref, q_ref, km_ref, vm_ref, lam_ref, gsub_ref,
         o_ref, m_ref, l_ref, acc_ref) = refs
    step = pl.program_id(0)
    qi = qi_tab[step]
    ki = ki_tab[step]
    is_last = last_tab[step]
    q0 = qpos_base + qi * tq

    lane = lax.broadcasted_iota(jnp.int32, (1, V_DIM), 1)
    first_map = lane < HEAD_DIM

    def stacked_queries(h):
        qh = q_ref[h]
        zero = jnp.zeros_like(qh)
        return jnp.concatenate([jnp.where(first_map, qh, zero), jnp.where(first_map, zero, qh)], axis=0)

    def scores(h, keys, key_pos):
        s = lax.dot_general(stacked_queries(h), keys, (((1,), (1,)), ((), ())), preferred_element_type=F32)
        return s + slopes_ref[h] * (key_pos - q0).astype(F32)

    def query_pos(n_keys):
        row = lax.broadcasted_iota(jnp.int32, (2 * tq, n_keys), 0)
        return q0 + (row & (tq - 1))

    def meta_body(h, carry):
        col = lax.broadcasted_iota(jnp.int32, (1, LANES), 1)
        s = scores(h, km_ref[h], col)
        colf = lax.broadcasted_iota(jnp.int32, (2 * tq, LANES), 1)
        valid = (colf < N_META) & (colf <= query_pos(LANES))
        _online_softmax_update(jnp.where(valid, s, NEG), vm_ref[h], m_ref, l_ref, acc_ref, h)
        return carry

    def main_body(h, carry, *, masked):
        key_pos = N_META + ki * tk + lax.broadcasted_iota(jnp.int32, (1, tk), 1)
        s = scores(h, k_ref[h], key_pos)
        if masked:
            kp = N_META + ki * tk + lax.broadcasted_iota(jnp.int32, (2 * tq, tk), 1)
            s = jnp.where(kp <= query_pos(tk), s, NEG)
        _online_softmax_update(s, v_ref[h], m_ref, l_ref, acc_ref, h)
        return carry

    def finalize():
        lam = _lambda_value(lam_ref)
        gsub = gsub_ref[...]
        for h in range(N_HEADS):
            inv_l = 1.0 / l_ref[h]
            acc = acc_ref[h] * inv_l
            o = acc[:tq] - lam * acc[tq:]
            o_ref[:, h * V_DIM:(h + 1) * V_DIM] = _sub_layer_norm(o, gsub).astype(o_ref.dtype)

    @pl.when(ki == 0)
    def _first():
        m_ref[...] = jnp.full(m_ref.shape, NEG, F32)
        l_ref[...] = jnp.zeros(l_ref.shape, F32)
        acc_ref[...] = jnp.zeros(acc_ref.shape, F32)
        lax.fori_loop(0, N_HEADS, meta_body, 0)

    if has_main:
        @pl.when(is_last == 0)
        def _interior():
            lax.fori_loop(0, N_HEADS, functools.partial(main_body, masked=False), 0)

        @pl.when(is_last == 1)
        def _diagonal():
            lax.fori_loop(0, N_HEADS, functools.partial(main_body, masked=True), 0)
            finalize()
    else:
        finalize()


def _attn_tables(n_q, tq, tk):
    qi, ki, last = [], [], []
    for q in range(n_q):
        n_k = ((q + 1) * tq - 1) // tk + 1
        for k in range(n_k):
            qi.append(q)
            ki.append(k)
            last.append(int(k == n_k - 1))
    return (np.asarray(qi, np.int32), np.asarray(ki, np.int32), np.asarray(last, np.int32))


def _prompt_attention(qh, kh, vh, kmeta, vmeta, slopes, lam_params, gsub, *, tq, tk, qpos_base):
    t_q = qh.shape[1]
    has_main = kh is not None
    if has_main:
        qi_t, ki_t, last_t = _attn_tables(t_q // tq, tq, tk)
    else:
        qi_t, ki_t, last_t = (np.zeros((1,), np.int32), np.zeros((1,), np.int32), np.ones((1,), np.int32))
    kern = functools.partial(_prompt_attn_kernel, tq=tq, tk=tk, has_main=has_main, qpos_base=qpos_base)
    in_specs = [
        pl.BlockSpec(memory_space=pltpu.SMEM),
        pl.BlockSpec((N_HEADS, tq, V_DIM), lambda s, qi, ki, la: (0, qi[s], 0)),
    ]
    args = [slopes, qh]
    if has_main:
        kv_spec = pl.BlockSpec((N_HEADS, tk, V_DIM), lambda s, qi, ki, la: (0, ki[s], 0))
        in_specs += [kv_spec, kv_spec]
        args += [kh, vh]
    in_specs += [
        _const_spec((N_HEADS, LANES, V_DIM)),
        _const_spec((N_HEADS, LANES, V_DIM)),
        _const_spec((4, HEAD_DIM)),
        _const_spec((1, V_DIM)),
    ]
    args += [kmeta, vmeta, lam_params, gsub]
    grid_spec = pltpu.PrefetchScalarGridSpec(
        num_scalar_prefetch=3,
        grid=(len(qi_t),),
        in_specs=in_specs,
        out_specs=pl.BlockSpec((tq, D_V), lambda s, qi, ki, la: (qi[s], 0)),
        scratch_shapes=[
            pltpu.VMEM((N_HEADS, 2 * tq, 1), F32),
            pltpu.VMEM((N_HEADS, 2 * tq, 1), F32),
            pltpu.VMEM((N_HEADS, 2 * tq, V_DIM), F32),
        ],
    )
    return pl.pallas_call(
        kern,
        grid_spec=grid_spec,
        out_shape=jax.ShapeDtypeStruct((t_q, D_V), BF16),
        compiler_params=_params(1),
        name="prompt_attn" if has_main else "meta_attn",
    )(jnp.asarray(qi_t), jnp.asarray(ki_t), jnp.asarray(last_t), *args)


def _sample_attn_kernel(*refs, pps, n_steps, past_len, n_q):
    pt_ref = refs[0]
    qbd_ref = refs[1]
    k_refs = refs[2:2 + pps]
    v_refs = refs[2 + pps:2 + 2 * pps]
    knew_ref, vnew_ref, slope_ref, lam_ref, gsub_ref, o_ref, m_ref, l_ref, acc_ref = refs[2 + 2 * pps:]
    del pt_ref
    j = pl.program_id(1)
    n_rows = N_HEADS * 2 * n_q
    qbd = qbd_ref[...]
    slope_rows = slope_ref[...]

    def scores(keys_bf16, rel_pos):
        s = lax.dot_general(qbd, keys_bf16, (((1,), (1,)), ((), ())), preferred_element_type=F32)
        return s + slope_rows * rel_pos.astype(F32)

    @pl.when(j == 0)
    def _init():
        m_ref[...] = jnp.full(m_ref.shape, NEG, F32)
        l_ref[...] = jnp.zeros(l_ref.shape, F32)
        acc_ref[...] = jnp.zeros(acc_ref.shape, F32)

    chunk = pps * PAGE_SIZE
    keys = jnp.concatenate([r[...].astype(BF16) for r in k_refs], axis=0)
    vals = jnp.concatenate([r[...].astype(BF16) for r in v_refs], axis=0)
    rel = j * chunk - past_len + lax.broadcasted_iota(jnp.int32, (1, chunk), 1)
    _online_softmax_update(scores(keys, rel), vals, m_ref, l_ref, acc_ref, 0)

    @pl.when(j == n_steps - 1)
    def _finish():
        col = lax.broadcasted_iota(jnp.int32, (1, LANES), 1)
        s = scores(knew_ref[...].astype(BF16), col)
        colf = lax.broadcasted_iota(jnp.int32, (n_rows, LANES), 1)
        row = lax.broadcasted_iota(jnp.int32, (n_rows, LANES), 0)
        valid = (colf < n_q) & (colf <= (row & (n_q - 1)))
        _online_softmax_update(jnp.where(valid, s, NEG), vnew_ref[...].astype(BF16), m_ref, l_ref, acc_ref, 0)

        lam = _lambda_value(lam_ref)
        gsub = gsub_ref[...]
        acc = acc_ref[0] * (1.0 / l_ref[0])
        for h in range(N_HEADS):
            r0 = h * 2 * n_q
            blk = acc[r0:r0 + 2 * n_q, h * V_DIM:(h + 1) * V_DIM]
            o = blk[:n_q] - lam * blk[n_q:]
            o_ref[:, h * V_DIM:(h + 1) * V_DIM] = _sub_layer_norm(o, gsub)


def _sample_attention(qbd, cache_k, cache_v, page_table, knew, vnew, slope_rows, lam_params, gsub, *, pps):
    n_seq, n_rows, _ = qbd.shape
    n_pages = page_table.shape[1]
    n_q = n_rows // (2 * N_HEADS)
    n_steps = n_pages // pps
    kern = functools.partial(_sample_attn_kernel, pps=pps, n_steps=n_steps,
                             past_len=n_pages * PAGE_SIZE, n_q=n_q)

    def page_spec(r):
        return pl.BlockSpec((None, PAGE_SIZE, D_QK),
                            lambda n, j, pt: (pt[n * n_pages + j * pps + r], 0, 0))

    seq_spec = lambda rows: pl.BlockSpec((None, rows, D_QK), lambda n, j, pt: (n, 0, 0))
    grid_spec = pltpu.PrefetchScalarGridSpec(
        num_scalar_prefetch=1,
        grid=(n_seq, n_steps),
        in_specs=([seq_spec(n_rows)]
                  + [page_spec(r) for r in range(pps)]
                  + [page_spec(r) for r in range(pps)]
                  + [seq_spec(LANES), seq_spec(LANES),
                     _const_spec((n_rows, 1)), _const_spec((4, HEAD_DIM)), _const_spec((1, V_DIM))]),
        out_specs=seq_spec(n_q),
        scratch_shapes=[
            pltpu.VMEM((1, n_rows, 1), F32),
            pltpu.VMEM((1, n_rows, 1), F32),
            pltpu.VMEM((1, n_rows, D_V), F32),
        ],
    )
    return pl.pallas_call(
        kern,
        grid_spec=grid_spec,
        out_shape=jax.ShapeDtypeStruct((n_seq, n_q, D_V), F32),
        compiler_params=_params(2),
        name="sample_attn",
    )(page_table.reshape(-1), qbd, *([cache_k] * pps), *([cache_v] * pps),
      knew, vnew, slope_rows, lam_params, gsub)


def kernel(x_prompt, x_sample, cache_k, cache_v, state_conv, state_ffn, page_table, meta_tokens, g_mix, g_ffn,
           w_pw1, w_dw, b_dw, ln_g, ln_b, w_pw2, g_kv, w_kv, g_k, w_q, g_q, lambda_params, g_subln, w_o,
           w_ffn_in, w_ffn_dw, b_ffn_dw, w_ffn_down):
    n_p, seq, _ = x_prompt.shape
    n_s, dec_seq, _ = x_sample.shape
    assert n_p == 1 and seq % ROW_TILE == 0 and seq % ATTN_TK == 0 and dec_seq == SUBLANES

    row = lambda a: a.reshape(1, -1).astype(F32)
    bf = lambda a: a.astype(BF16)
    wpw1, wpw2, wkv, wq, wo = bf(w_pw1[0]), bf(w_pw2[0]), bf(w_kv), bf(w_q[0]), bf(w_o[0])
    win, wdown = bf(w_ffn_in), bf(w_ffn_down)
    gk_t = jnp.tile(g_k.reshape(-1), N_HEADS).reshape(1, D_QK)
    gq_t = jnp.tile(g_q[0].reshape(-1), N_HEADS).reshape(1, D_QK)
    grp = np.arange(MXU_DIM) // HEAD_DIM
    ones_blk = jnp.asarray(grp[:, None] == grp[None, :], BF16)
    slopes = jnp.exp2(-8.0 * jnp.arange(1, N_HEADS + 1, dtype=F32) / N_HEADS)
    lam_p = lambda_params[0].astype(F32)
    gsub = row(g_subln[0])

    def layer0(h, conv_past, ffn_past, nb, tm, tm_ffn):
        h1, conv_state = _conv_module(h, conv_past, row(g_mix[0]), wpw1, w_dw[0], row(b_dw[0]),
                                      row(ln_g[0]), row(ln_b[0]), wpw2, nb=nb, tm=tm)
        h2, ffn_state = _ffn(h1, ffn_past, row(g_ffn[0]), win[0], w_ffn_dw[0], row(b_ffn_dw[0]), wdown[0],
                             nb=nb, tm=tm_ffn)
        return h2, conv_state, ffn_state

    def project(h2d, tm):
        return _kvq(h2d, row(g_kv), wkv, gk_t, row(g_mix[1]), wq, gq_t, ones_blk, tm=tm)

    def layer1_ffn(h, o, ffn_past, nb, tm):
        return _ffn(h, ffn_past, row(g_ffn[1]), win[1], w_ffn_dw[1], row(b_ffn_dw[1]), wdown[1],
                    nb=nb, tm=tm, o=o, wo=wo)

    h_m = meta_tokens.astype(F32)[None]
    zeros_conv = jnp.zeros((1, CONV_HALO, D_CONV), F32)
    zeros_ffn = jnp.zeros((1, FFN_HALO, D_FF), F32)
    h2_m, conv_m, ffn0_m = layer0(h_m, zeros_conv, zeros_ffn, 1, N_META, N_META)
    k_m, v_m, kh_m, vh_m, qh_m = project(h2_m[0], N_META)
    pad_keys = lambda a: jnp.pad(a, ((0, 0), (0, LANES - N_META), (0, 0)))
    kh_m, vh_m = pad_keys(kh_m), pad_keys(vh_m)
    o_m = _prompt_attention(qh_m, None, None, kh_m, vh_m, slopes, lam_p, gsub,
                            tq=N_META, tk=ATTN_TK, qpos_base=0)
    _, ffn1_m = layer1_ffn(h2_m, o_m[None], zeros_ffn, 1, N_META)

    h2_p, conv_p, ffn0_p = layer0(x_prompt, conv_m, ffn0_m, 1, ROW_TILE, FFN_ROW_TILE)
    k_p, v_p, kh_p, vh_p, qh_p = project(h2_p[0], ROW_TILE)
    o_p = _prompt_attention(qh_p, kh_p, vh_p, kh_m, vh_m, slopes, lam_p, gsub,
                            tq=ATTN_TQ, tk=ATTN_TK, qpos_base=N_META)
    y_p, ffn1_p = layer1_ffn(h2_p, o_p[None], ffn1_m, 1, FFN_ROW_TILE)

    h2_s, conv_s, ffn0_s = layer0(x_sample, state_conv[0], state_ffn[0], n_s, dec_seq, dec_seq)
    k_s, v_s, _, _, qh_s = project(h2_s.reshape(n_s * dec_seq, D_MODEL), n_s * dec_seq)
    n_grp = 2 * N_HEADS
    q_g = qh_s.reshape(N_HEADS, n_s, dec_seq, 2, HEAD_DIM).transpose(1, 0, 3, 2, 4)
    q_g = q_g.reshape(n_s, n_grp, dec_seq, HEAD_DIM)
    eye = jnp.eye(n_grp, dtype=BF16)
    qbd = (q_g[:, :, :, None, :] * eye[None, :, None, :, None]).reshape(n_s, n_grp * dec_seq, D_QK)
    slope_rows = jnp.repeat(slopes, 2 * dec_seq).reshape(n_grp * dec_seq, 1)
    pad_new = lambda a: jnp.pad(a.reshape(n_s, dec_seq, D_QK), ((0, 0), (0, LANES - dec_seq), (0, 0)))
    n_pool = cache_k.shape[0]
    o_s = _sample_attention(qbd, cache_k.reshape(n_pool, PAGE_SIZE, D_QK), cache_v.reshape(n_pool, PAGE_SIZE, D_V),
                            page_table, pad_new(k_s), pad_new(v_s), slope_rows, lam_p, gsub,
                            pps=PAGES_PER_STEP)
    y_s, ffn1_s = layer1_ffn(h2_s, o_s, state_ffn[1], n_s, dec_seq)

    t_all = N_META + seq
    k_prompt = jnp.concatenate([k_m, k_p], axis=0).reshape(1, t_all, N_HEADS, 2 * HEAD_DIM)
    v_prompt = jnp.concatenate([v_m, v_p], axis=0).reshape(1, t_all, N_HEADS, V_DIM)
    return (y_p, y_s, k_prompt, v_prompt,
            k_s.reshape(n_s, dec_seq, N_HEADS, 2 * HEAD_DIM), v_s.reshape(n_s, dec_seq, N_HEADS, V_DIM),
            conv_p[None], conv_s[None],
            jnp.stack([ffn0_p, ffn1_p]), jnp.stack([ffn0_s, ffn1_s]))
```

```python
import functools
import math

import numpy as np
import jax
import jax.numpy as jnp
from jax import lax
from jax.experimental import pallas as pl
from jax.experimental.pallas import tpu as pltpu

F32 = jnp.float32
BF16 = jnp.bfloat16

D_MODEL = 1024
N_META = 16
CONV_WIDTH = 31
D_CONV = D_MODEL
N_HEADS = 8
HEAD_DIM = 64
V_DIM = 2 * HEAD_DIM
D_QK = N_HEADS * 2 * HEAD_DIM
D_V = N_HEADS * V_DIM
D_FF = ((8 * D_MODEL // 3 + 127) // 128) * 128
FFN_CONV_WIDTH = 3
PAGE_SIZE = 128
EPS = 1e-6
NEG = -1e30
LAMBDA_INIT_L1 = 0.8 - 0.6 * math.exp(-0.3 * 1)
LOG2_E = math.log2(math.e)

LANES = 128
SUBLANES = 8
MXU_DIM = 256
VMEM_LIMIT_BYTES = 56 * 1024 * 1024

CONV_HALO = CONV_WIDTH - 1
CONV_HALO_PAD = 32
FFN_HALO = FFN_CONV_WIDTH - 1
FFN_HALO_PAD = SUBLANES

ROW_TILE = 512
FFN_ROW_TILE = 256
ATTN_TQ = 512
ATTN_TK = 1024
PAGES_PER_STEP = 8

J_SPLIT = 32
N_SLOPE_TERMS = 3


def _const_spec(shape):
    zeros = (0,) * len(shape)
    return pl.BlockSpec(shape, lambda *_: zeros, pipeline_mode=pl.Buffered(1))


def _params(n_axes):
    return pltpu.CompilerParams(
        dimension_semantics=("arbitrary",) * n_axes,
        vmem_limit_bytes=VMEM_LIMIT_BYTES,
    )


def _rms_norm(x, g):
    return x * lax.rsqrt(jnp.mean(x * x, axis=-1, keepdims=True) + EPS) * g


def _silu(x):
    return x * jax.nn.sigmoid(x)


def _conv_module_kernel(h_ref, past_ref, g_ref, wpw1_ref, wdw_ref, bdw_ref, lng_ref, lnb_ref, wpw2_ref,
                        out_ref, state_ref, buf_ref, *, nb, tm):
    lo = CONV_HALO_PAD - CONV_HALO

    @pl.when(pl.program_id(1) == 0)
    def _load_history():
        buf_ref[:, lo:CONV_HALO_PAD, :] = past_ref[...]

    x = h_ref[...].reshape(nb * tm, D_MODEL)
    u = _rms_norm(x, g_ref[...]).astype(BF16)
    pg = jnp.dot(u, wpw1_ref[...], preferred_element_type=F32)
    glu = pg[:, :D_CONV] * jax.nn.sigmoid(pg[:, D_CONV:])
    buf_ref[:, CONV_HALO_PAD:CONV_HALO_PAD + tm, :] = glu.reshape(nb, tm, D_CONV)

    acc = jnp.broadcast_to(bdw_ref[...].reshape(1, 1, D_CONV), (nb, tm, D_CONV))
    for w in range(CONV_WIDTH):
        acc = acc + wdw_ref[w:w + 1, :].reshape(1, 1, D_CONV) * buf_ref[:, lo + w:lo + w + tm, :]

    new_state = buf_ref[:, tm + lo:tm + CONV_HALO_PAD, :]
    state_ref[...] = new_state
    buf_ref[:, lo:CONV_HALO_PAD, :] = new_state

    c = acc.reshape(nb * tm, D_CONV)
    mu = jnp.mean(c, axis=-1, keepdims=True)
    cc = c - mu
    c = cc * lax.rsqrt(jnp.mean(cc * cc, axis=-1, keepdims=True) + EPS) * lng_ref[...] + lnb_ref[...]
    y = jnp.dot(_silu(c).astype(BF16), wpw2_ref[...], preferred_element_type=F32)
    out_ref[...] = (x + y).reshape(nb, tm, D_MODEL)


def _conv_module(h, past, g, wpw1, wdw, bdw, lng, lnb, wpw2, *, nb, tm):
    ns, length, _ = h.shape
    grid = (ns // nb, length // tm)
    kern = functools.partial(_conv_module_kernel, nb=nb, tm=tm)
    return pl.pallas_call(
        kern,
        grid=grid,
        in_specs=[
            pl.BlockSpec((nb, tm, D_MODEL), lambda s, t: (s, t, 0)),
            pl.BlockSpec((nb, CONV_HALO, D_CONV), lambda s, t: (s, 0, 0)),
            _const_spec((1, D_MODEL)),
            _const_spec((D_MODEL, 2 * D_CONV)),
            _const_spec((CONV_WIDTH, D_CONV)),
            _const_spec((1, D_CONV)),
            _const_spec((1, D_CONV)),
            _const_spec((1, D_CONV)),
            _const_spec((D_CONV, D_MODEL)),
        ],
        out_specs=[
            pl.BlockSpec((nb, tm, D_MODEL), lambda s, t: (s, t, 0)),
            pl.BlockSpec((nb, CONV_HALO, D_CONV), lambda s, t: (s, 0, 0)),
        ],
        out_shape=[
            jax.ShapeDtypeStruct((ns, length, D_MODEL), F32),
            jax.ShapeDtypeStruct((ns, CONV_HALO, D_CONV), F32),
        ],
        scratch_shapes=[pltpu.VMEM((nb, CONV_HALO_PAD + tm, D_CONV), F32)],
        compiler_params=_params(2),
        name="conv_module",
    )(h, past, g, wpw1, wdw, bdw, lng, lnb, wpw2)


def _ffn_kernel(*refs, nb, tm, with_proj):
    if with_proj:
        (h_ref, o_ref, wo_ref, past_ref, g_ref, win_ref, wdw_ref, bdw_ref, wdown_ref,
         out_ref, state_ref, buf_ref) = refs
    else:
        (h_ref, past_ref, g_ref, win_ref, wdw_ref, bdw_ref, wdown_ref,
         out_ref, state_ref, buf_ref) = refs
    lo = FFN_HALO_PAD - FFN_HALO

    @pl.when(pl.program_id(1) == 0)
    def _load_history():
        buf_ref[:, lo:FFN_HALO_PAD, :] = past_ref[...]

    x = h_ref[...].reshape(nb * tm, D_MODEL)
    if with_proj:
        o = o_ref[...].reshape(nb * tm, D_V).astype(BF16)
        x = x + jnp.dot(o, wo_ref[...], preferred_element_type=F32)
    u = _rms_norm(x, g_ref[...]).astype(BF16)
    ag = jnp.dot(u, win_ref[...], preferred_element_type=F32)
    a = ag[:, :D_FF].reshape(nb, tm, D_FF)
    gate = ag[:, D_FF:]
    buf_ref[:, FFN_HALO_PAD:FFN_HALO_PAD + tm, :] = a

    conv = (bdw_ref[...].reshape(1, 1, D_FF)
            + wdw_ref[0:1, :].reshape(1, 1, D_FF) * buf_ref[:, lo:lo + tm, :]
            + wdw_ref[1:2, :].reshape(1, 1, D_FF) * buf_ref[:, lo + 1:lo + 1 + tm, :]
            + wdw_ref[2:3, :].reshape(1, 1, D_FF) * a)

    new_state = buf_ref[:, tm + lo:tm + FFN_HALO_PAD, :]
    state_ref[...] = new_state
    buf_ref[:, lo:FFN_HALO_PAD, :] = new_state

    act = (_silu(conv.reshape(nb * tm, D_FF)) * gate).astype(BF16)
    f = jnp.dot(act, wdown_ref[...], preferred_element_type=F32)
    out_ref[...] = (x + f).reshape(nb, tm, D_MODEL)


def _ffn(h, past, g, win, wdw, bdw, wdown, *, nb, tm, o=None, wo=None):
    ns, length, _ = h.shape
    grid = (ns // nb, length // tm)
    with_proj = o is not None
    kern = functools.partial(_ffn_kernel, nb=nb, tm=tm, with_proj=with_proj)
    row_spec = pl.BlockSpec((nb, tm, D_MODEL), lambda s, t: (s, t, 0))
    state_spec = pl.BlockSpec((nb, FFN_HALO, D_FF), lambda s, t: (s, 0, 0))
    in_specs = [row_spec]
    args = [h]
    if with_proj:
        in_specs += [pl.BlockSpec((nb, tm, D_V), lambda s, t: (s, t, 0)), _const_spec((D_V, D_MODEL))]
        args += [o, wo]
    in_specs += [
        state_spec,
        _const_spec((1, D_MODEL)),
        _const_spec((D_MODEL, 2 * D_FF)),
        _const_spec((FFN_CONV_WIDTH, D_FF)),
        _const_spec((1, D_FF)),
        _const_spec((D_FF, D_MODEL)),
    ]
    args += [past, g, win, wdw, bdw, wdown]
    return pl.pallas_call(
        kern,
        grid=grid,
        in_specs=in_specs,
        out_specs=[row_spec, state_spec],
        out_shape=[
            jax.ShapeDtypeStruct((ns, length, D_MODEL), F32),
            jax.ShapeDtypeStruct((ns, FFN_HALO, D_FF), F32),
        ],
        scratch_shapes=[pltpu.VMEM((nb, FFN_HALO_PAD + tm, D_FF), F32)],
        compiler_params=_params(2),
        name="conv_ffn_proj" if with_proj else "conv_ffn",
    )(*args)


def _group_rms_norm(x, gain, ones_blk):
    x2 = x * x
    hi = x2.astype(BF16)
    lo = (x2 - hi.astype(F32)).astype(BF16)
    parts = []
    for j in range(x.shape[-1] // MXU_DIM):
        sl = slice(j * MXU_DIM, (j + 1) * MXU_DIM)
        parts.append(jnp.dot(hi[:, sl], ones_blk, preferred_element_type=F32)
                     + jnp.dot(lo[:, sl], ones_blk, preferred_element_type=F32))
    ss = jnp.concatenate(parts, axis=1)
    return x * lax.rsqrt(ss * (1.0 / HEAD_DIM) + EPS) * gain


def _kvq_kernel(h_ref, gkv_ref, wkv_ref, gk_ref, gmix_ref, wq_ref, gq_ref, ones_ref,
                k_ref, v_ref, kh_ref, vh_ref, qh_ref):
    x = h_ref[...]
    ones_blk = ones_ref[...]
    kv = jnp.dot(_rms_norm(x, gkv_ref[...]).astype(BF16), wkv_ref[...], preferred_element_type=F32)
    k = _group_rms_norm(kv[:, :D_QK], gk_ref[...], ones_blk)
    v = kv[:, D_QK:]
    q = jnp.dot(_rms_norm(x, gmix_ref[...]).astype(BF16), wq_ref[...], preferred_element_type=F32)
    q = _group_rms_norm(q, gq_ref[...], ones_blk) * (HEAD_DIM ** -0.5 * LOG2_E)
    k_ref[...] = k
    v_ref[...] = v
    kb, vb, qb = k.astype(BF16), v.astype(BF16), q.astype(BF16)
    for h in range(N_HEADS):
        sl = slice(h * V_DIM, (h + 1) * V_DIM)
        kh_ref[h] = kb[:, sl]
        vh_ref[h] = vb[:, sl]
        qh_ref[h] = qb[:, sl]


def _kvq(h2d, gkv, wkv, gk_t, gmix, wq, gq_t, ones_blk, *, tm):
    rows = h2d.shape[0]
    row_spec = pl.BlockSpec((tm, D_MODEL), lambda t: (t, 0))
    head_spec = pl.BlockSpec((N_HEADS, tm, V_DIM), lambda t: (0, t, 0))
    return pl.pallas_call(
        _kvq_kernel,
        grid=(rows // tm,),
        in_specs=[
            row_spec,
            _const_spec((1, D_MODEL)),
            _const_spec((D_MODEL, D_QK + D_V)),
            _const_spec((1, D_QK)),
            _const_spec((1, D_MODEL)),
            _const_spec((D_MODEL, D_QK)),
            _const_spec((1, D_QK)),
            _const_spec((MXU_DIM, MXU_DIM)),
        ],
        out_specs=[row_spec, row_spec, head_spec, head_spec, head_spec],
        out_shape=[
            jax.ShapeDtypeStruct((rows, D_QK), F32),
            jax.ShapeDtypeStruct((rows, D_V), F32),
            jax.ShapeDtypeStruct((N_HEADS, rows, V_DIM), BF16),
            jax.ShapeDtypeStruct((N_HEADS, rows, V_DIM), BF16),
            jax.ShapeDtypeStruct((N_HEADS, rows, V_DIM), BF16),
        ],
        compiler_params=_params(1),
        name="kvq_proj",
    )(h2d, gkv, wkv, gk_t, gmix, wq, gq_t, ones_blk)


def _lambda_value(lam_ref):
    lp = lam_ref[...]
    a = jnp.sum(lp[0:1, :] * lp[1:2, :], axis=-1, keepdims=True)
    b = jnp.sum(lp[2:3, :] * lp[3:4, :], axis=-1, keepdims=True)
    return jnp.exp(a) - jnp.exp(b) + LAMBDA_INIT_L1


def _sub_layer_norm(o, gsub):
    o = o * lax.rsqrt(jnp.mean(o * o, axis=-1, keepdims=True) + EPS) * gsub
    return o * (1.0 - LAMBDA_INIT_L1)


def _int_to_f32(x):
    return (x + jnp.zeros((1, 1), jnp.int32)).astype(F32)


def _log2(n):
    assert n > 0 and n & (n - 1) == 0, n
    return n.bit_length() - 1


FIRST, INTERIOR, DIAGONAL = 0, 1, 2
MASKED_POS = 1 << 30


def _stacked_queries(q_ref, qaux_ref, h, tq):
    qh = q_ref[h]
    first_map = lax.broadcasted_iota(jnp.int32, (1, V_DIM), 1) < HEAD_DIM
    zero = jnp.zeros_like(qh)
    qq = jnp.concatenate([jnp.where(first_map, qh, zero), jnp.where(first_map, zero, qh)], axis=0)
    aux = jnp.broadcast_to(qaux_ref[h], (2 * tq, LANES)).astype(BF16)
    return jnp.concatenate([qq, aux], axis=1)


def _query_pos(q0, tq, n_keys):
    row = lax.broadcasted_iota(jnp.int32, (2 * tq, n_keys), 0)
    return q0 + (row & (tq - 1))


def _nt_dot(a, b):
    return lax.dot_general(a, b, (((1,), (1,)), ((), ())), preferred_element_type=F32)


def _finalize_heads(acc_ref, lam_ref, gsub_ref, o_ref, tq):
    lam = _lambda_value(lam_ref)
    gsub = gsub_ref[...]
    for h in range(N_HEADS):
        acc = acc_ref[h]
        num = acc[:, :V_DIM] * (1.0 / acc[:, V_DIM:V_DIM + 1])
        o = num[:tq] - lam * num[tq:]
        o_ref[:, h * V_DIM:(h + 1) * V_DIM] = _sub_layer_norm(o, gsub).astype(o_ref.dtype)


def _meta_attn_kernel(q_ref, km_ref, vm_ref, qaux_ref, kaux_ref, vaux_ref, lam_ref, gsub_ref, o_ref, acc_ref, *, tq):
    def body(h, carry):
        kk = jnp.concatenate([km_ref[h], kaux_ref[...]], axis=1)
        s = _nt_dot(_stacked_queries(q_ref, qaux_ref, h, tq), kk)
        col = lax.broadcasted_iota(jnp.int32, (1, LANES), 1)
        s = jnp.where(col <= _query_pos(0, tq, LANES), s, NEG)
        p = jnp.exp2(s - jnp.max(s, axis=-1, keepdims=True))
        vv = jnp.concatenate([vm_ref[h], vaux_ref[...]], axis=1)
        acc_ref[h] = jnp.dot(p.astype(BF16), vv, preferred_element_type=F32)
        return carry

    lax.fori_loop(0, N_HEADS, body, 0)
    _finalize_heads(acc_ref, lam_ref, gsub_ref, o_ref, tq)


def _prompt_attn_kernel(qi_tab, ki_tab, kind_tab, last_tab, c_ref, q_ref, k_ref, v_ref, km_ref, vm_ref, qaux_ref,
                        kauxf_ref, kauxm_ref, vaux_ref, lam_ref, gsub_ref, o_ref, m_ref, acc_ref, s0_ref, s1_ref,
                        *, tq, tk):
    step = pl.program_id(0)
    q0 = N_META + qi_tab[step] * tq
    k0 = N_META + ki_tab[step] * tk

    def qk(h, s_ref, kind):
        q2 = _stacked_queries(q_ref, qaux_ref, h, tq)
        if kind == FIRST:
            keys = jnp.concatenate([km_ref[h], k_ref[h]], axis=0)
            s = _nt_dot(q2, jnp.concatenate([keys, kauxf_ref[...]], axis=1))
            col = lax.broadcasted_iota(jnp.int32, (1, LANES + tk), 1)
            kpos = jnp.where(col < LANES, jnp.where(col < N_META, col, MASKED_POS), col + (N_META - LANES))
            s_ref[...] = jnp.where(kpos <= _query_pos(q0, tq, LANES + tk), s, NEG)
            return
        s = _nt_dot(q2, jnp.concatenate([k_ref[h], kauxm_ref[...]], axis=1))
        if kind == DIAGONAL:
            kpos = k0 + lax.broadcasted_iota(jnp.int32, (1, tk), 1)
            s = jnp.where(kpos <= _query_pos(q0, tq, tk), s, NEG)
        s_ref[:, 0:tk] = s

    def softmax_pv(h, s_ref, kind):
        if kind == FIRST:
            s, vals, n_keys, key0 = s_ref[...], jnp.concatenate([vm_ref[h], v_ref[h]], axis=0), LANES + tk, 0
        else:
            s, vals, n_keys, key0 = s_ref[:, 0:tk], v_ref[h], tk, k0
        vv = jnp.concatenate([vals, vaux_ref[0:n_keys, :]], axis=1)
        c = c_ref[h] * _int_to_f32(key0 - q0)
        m_new = jnp.max(s, axis=-1, keepdims=True) + c
        if kind == FIRST:
            p = jnp.exp2(s - (m_new - c))
            acc_ref[h] = jnp.dot(p.astype(BF16), vv, preferred_element_type=F32)
        else:
            m_old = m_ref[h]
            m_new = jnp.maximum(m_old, m_new)
            p = jnp.exp2(s - (m_new - c))
            acc_ref[h] = (jnp.exp2(m_old - m_new) * acc_ref[h]
                          + jnp.dot(p.astype(BF16), vv, preferred_element_type=F32))
        m_ref[h] = m_new

    def run(kind):
        qk(0, s0_ref, kind)

        def body(i, carry):
            a = 2 * i
            qk(a + 1, s1_ref, kind)
            softmax_pv(a, s0_ref, kind)
            qk(a + 2, s0_ref, kind)
            softmax_pv(a + 1, s1_ref, kind)
            return carry

        lax.fori_loop(0, N_HEADS // 2 - 1, body, 0)
        qk(N_HEADS - 1, s1_ref, kind)
        softmax_pv(N_HEADS - 2, s0_ref, kind)
        softmax_pv(N_HEADS - 1, s1_ref, kind)

    for kind in (FIRST, INTERIOR, DIAGONAL):
        pl.when(kind_tab[step] == kind)(functools.partial(run, kind))

    @pl.when(last_tab[step] == 1)
    def _finish():
        _finalize_heads(acc_ref, lam_ref, gsub_ref, o_ref, tq)


def _attn_tables(n_q, tq, tk):
    qi, ki, kind, last = [], [], [], []
    for q in range(n_q):
        n_k = ((q + 1) * tq - 1) // tk + 1
        for k in range(n_k):
            qi.append(q)
            ki.append(k)
            last.append(int(k == n_k - 1))
            kind.append(FIRST if k == 0 else (DIAGONAL if k == n_k - 1 else INTERIOR))
    return tuple(jnp.asarray(np.asarray(t, np.int32)) for t in (qi, ki, kind, last))


def _split_bf16(x, n_terms):
    terms, rest = [], x
    for _ in range(n_terms):
        t = rest.astype(BF16).astype(F32)
        terms.append(t)
        rest = rest - t
    return terms


def _alibi_aux(c_log2, tk):
    terms = _split_bf16(c_log2, N_SLOPE_TERMS)
    qaux = jnp.zeros((N_HEADS, 1, LANES), F32)
    for t, term in enumerate(terms):
        qaux = qaux.at[:, 0, t].set(term).at[:, 0, N_SLOPE_TERMS + t].set(term)

    def key_lanes(j):
        lanes = np.zeros((j.shape[0], LANES), np.float32)
        for t in range(N_SLOPE_TERMS):
            lanes[:, t] = (j // J_SPLIT) * J_SPLIT
            lanes[:, N_SLOPE_TERMS + t] = j % J_SPLIT
        return jnp.asarray(lanes, BF16)

    j_tile = np.arange(tk)
    j_first = np.concatenate([np.arange(LANES), N_META + j_tile])
    vaux = np.zeros((LANES + tk, LANES), np.float32)
    vaux[:, 0] = 1.0
    return qaux, key_lanes(j_first), key_lanes(j_tile), jnp.asarray(vaux, BF16)


def _meta_attention(qh, kmeta, vmeta, aux, lam_params, gsub):
    tq = qh.shape[1]
    qaux, _, kauxm, vaux = aux
    return pl.pallas_call(
        functools.partial(_meta_attn_kernel, tq=tq),
        grid=(1,),
        in_specs=[
            _const_spec((N_HEADS, tq, V_DIM)),
            _const_spec((N_HEADS, LANES, V_DIM)),
            _const_spec((N_HEADS, LANES, V_DIM)),
            _const_spec((N_HEADS, 1, LANES)),
            _const_spec((LANES, LANES)),
            _const_spec((LANES, LANES)),
            _const_spec((4, HEAD_DIM)),
            _const_spec((1, V_DIM)),
        ],
        out_specs=pl.BlockSpec((tq, D_V), lambda i: (0, 0)),
        out_shape=jax.ShapeDtypeStruct((tq, D_V), BF16),
        scratch_shapes=[pltpu.VMEM((N_HEADS, 2 * tq, 2 * V_DIM), F32)],
        compiler_params=_params(1),
        name="meta_attn",
    )(qh, kmeta, vmeta, qaux, kauxm[:LANES], vaux[:LANES], lam_params, gsub)


def _prompt_attention(qh, kh, vh, kmeta, vmeta, c_log2, aux, lam_params, gsub, *, tq, tk):
    t_q = qh.shape[1]
    qaux, kauxf, kauxm, vaux = aux
    tables = _attn_tables(t_q // tq, tq, tk)
    q_spec = pl.BlockSpec((N_HEADS, tq, V_DIM), lambda s, qi, ki, kd, la: (0, qi[s], 0))
    kv_spec = pl.BlockSpec((N_HEADS, tk, V_DIM), lambda s, qi, ki, kd, la: (0, ki[s], 0))
    grid_spec = pltpu.PrefetchScalarGridSpec(
        num_scalar_prefetch=4,
        grid=(tables[0].shape[0],),
        in_specs=[
            pl.BlockSpec(memory_space=pltpu.SMEM),
            q_spec, kv_spec, kv_spec,
            _const_spec((N_HEADS, LANES, V_DIM)),
            _const_spec((N_HEADS, LANES, V_DIM)),
            _const_spec((N_HEADS, 1, LANES)),
            _const_spec((LANES + tk, LANES)),
            _const_spec((tk, LANES)),
            _const_spec((LANES + tk, LANES)),
            _const_spec((4, HEAD_DIM)),
            _const_spec((1, V_DIM)),
        ],
        out_specs=pl.BlockSpec((tq, D_V), lambda s, qi, ki, kd, la: (qi[s], 0)),
        scratch_shapes=[
            pltpu.VMEM((N_HEADS, 2 * tq, 1), F32),
            pltpu.VMEM((N_HEADS, 2 * tq, 2 * V_DIM), F32),
            pltpu.VMEM((2 * tq, LANES + tk), F32),
            pltpu.VMEM((2 * tq, LANES + tk), F32),
        ],
    )
    return pl.pallas_call(
        functools.partial(_prompt_attn_kernel, tq=tq, tk=tk),
        grid_spec=grid_spec,
        out_shape=jax.ShapeDtypeStruct((t_q, D_V), BF16),
        compiler_params=_params(1),
        name="prompt_attn",
    )(*tables, c_log2, qh, kh, vh, kmeta, vmeta, qaux, kauxf, kauxm, vaux, lam_params, gsub)


def _sample_attn_kernel(*refs, pps, n_steps, past_len, n_q):
    wq_ref = refs[1]
    k_refs = refs[2:2 + pps]
    v_refs = refs[2 + pps:2 + 2 * pps]
    knew_ref, vnew_ref, bias_ref, c_ref, lam_ref, gsub_ref, o_ref, m_ref, l_ref, acc_ref = refs[2 + 2 * pps:]
    j = pl.program_id(1)
    n_rows = N_HEADS * 2 * n_q
    page_rows = PAGE_SIZE * N_HEADS
    wq = wq_ref[...]
    c_rows = c_ref[...]

    def update(keys, vals, s_bias, c):
        s = lax.dot_general(wq, keys, (((1,), (1,)), ((), ())), preferred_element_type=F32) + s_bias
        m_old = m_ref[...]
        m_new = jnp.maximum(m_old, jnp.max(s, axis=-1, keepdims=True) + c)
        alpha = jnp.exp2(m_old - m_new)
        p = jnp.exp2(s - (m_new - c))
        l_ref[...] = alpha * l_ref[...] + jnp.sum(p, axis=-1, keepdims=True)
        acc_ref[...] = alpha * acc_ref[...] + jnp.dot(p.astype(BF16), vals, preferred_element_type=F32)
        m_ref[...] = m_new

    @pl.when(j == 0)
    def _init():
        m_ref[...] = jnp.full(m_ref.shape, NEG, F32)
        l_ref[...] = jnp.zeros(l_ref.shape, F32)
        acc_ref[...] = jnp.zeros(acc_ref.shape, F32)

    as_rows = lambda r: r[...].reshape(page_rows, V_DIM).astype(BF16)
    keys = jnp.concatenate([as_rows(r) for r in k_refs], axis=0)
    vals = jnp.concatenate([as_rows(r) for r in v_refs], axis=0)
    update(keys, vals, bias_ref[...], c_rows * _int_to_f32(j * (pps * PAGE_SIZE) - past_len))

    @pl.when(j == n_steps - 1)
    def _finish():
        n_new = n_q * N_HEADS
        col = lax.broadcasted_iota(jnp.int32, (n_rows, n_new), 1)
        row = lax.broadcasted_iota(jnp.int32, (n_rows, n_new), 0)
        key = col >> _log2(N_HEADS)
        valid = ((col & (N_HEADS - 1)) == (row >> _log2(2 * n_q))) & (key <= (row & (n_q - 1)))
        s_bias = jnp.where(valid, c_rows * key.astype(F32), NEG)
        update(knew_ref[...].astype(BF16), vnew_ref[...].astype(BF16), s_bias, jnp.zeros((1, 1), F32))

        lam = _lambda_value(lam_ref)
        gsub = gsub_ref[...]
        acc = acc_ref[...] * (1.0 / l_ref[...])
        for h in range(N_HEADS):
            r0 = h * 2 * n_q
            o = acc[r0:r0 + n_q] - lam * acc[r0 + n_q:r0 + 2 * n_q]
            o_ref[:, h * V_DIM:(h + 1) * V_DIM] = _sub_layer_norm(o, gsub)


def _sample_attention(wq, cache_k, cache_v, page_table, knew, vnew, bias, c_rows, lam_params, gsub, *, pps):
    n_seq, n_rows, _ = wq.shape
    n_pages = page_table.shape[1]
    n_q = n_rows // (2 * N_HEADS)
    n_steps = n_pages // pps
    kern = functools.partial(_sample_attn_kernel, pps=pps, n_steps=n_steps,
                             past_len=n_pages * PAGE_SIZE, n_q=n_q)

    def page_spec(r):
        return pl.BlockSpec((None, PAGE_SIZE, N_HEADS, V_DIM),
                            lambda n, j, pt: (pt[n * n_pages + j * pps + r], 0, 0, 0))

    seq_spec = lambda rows, cols: pl.BlockSpec((None, rows, cols), lambda n, j, pt: (n, 0, 0))
    grid_spec = pltpu.PrefetchScalarGridSpec(
        num_scalar_prefetch=1,
        grid=(n_seq, n_steps),
        in_specs=([seq_spec(n_rows, V_DIM)]
                  + [page_spec(r) for r in range(pps)]
                  + [page_spec(r) for r in range(pps)]
                  + [seq_spec(n_q * N_HEADS, V_DIM), seq_spec(n_q * N_HEADS, V_DIM),
                     _const_spec(bias.shape), _const_spec((n_rows, 1)),
                     _const_spec((4, HEAD_DIM)), _const_spec((1, V_DIM))]),
        out_specs=seq_spec(n_q, D_V),
        scratch_shapes=[
            pltpu.VMEM((n_rows, 1), F32),
            pltpu.VMEM((n_rows, 1), F32),
            pltpu.VMEM((n_rows, V_DIM), F32),
        ],
    )
    return pl.pallas_call(
        kern,
        grid_spec=grid_spec,
        out_shape=jax.ShapeDtypeStruct((n_seq, n_q, D_V), F32),
        compiler_params=_params(2),
        name="sample_attn",
    )(page_table.reshape(-1), wq, *([cache_k] * pps), *([cache_v] * pps),
      knew, vnew, bias, c_rows, lam_params, gsub)


def kernel(x_prompt, x_sample, cache_k, cache_v, state_conv, state_ffn, page_table, meta_tokens, g_mix, g_ffn,
           w_pw1, w_dw, b_dw, ln_g, ln_b, w_pw2, g_kv, w_kv, g_k, w_q, g_q, lambda_params, g_subln, w_o,
           w_ffn_in, w_ffn_dw, b_ffn_dw, w_ffn_down):
    n_p, seq, _ = x_prompt.shape
    n_s, dec_seq, _ = x_sample.shape
    assert n_p == 1 and seq % ROW_TILE == 0 and seq % ATTN_TK == 0 and dec_seq == SUBLANES
    assert cache_k.shape[1:] == (PAGE_SIZE, N_HEADS, 2 * HEAD_DIM) and cache_v.shape[1:] == (PAGE_SIZE, N_HEADS, V_DIM)

    row = lambda a: a.reshape(1, -1).astype(F32)
    bf = lambda a: a.astype(BF16)
    wpw1, wpw2, wkv, wq, wo = bf(w_pw1[0]), bf(w_pw2[0]), bf(w_kv), bf(w_q[0]), bf(w_o[0])
    win, wdown = bf(w_ffn_in), bf(w_ffn_down)
    gk_t = jnp.tile(g_k.reshape(-1), N_HEADS).reshape(1, D_QK)
    gq_t = jnp.tile(g_q[0].reshape(-1), N_HEADS).reshape(1, D_QK)
    grp = np.arange(MXU_DIM) // HEAD_DIM
    ones_blk = jnp.asarray(grp[:, None] == grp[None, :], BF16)
    slopes = jnp.exp2(-8.0 * jnp.arange(1, N_HEADS + 1, dtype=F32) / N_HEADS)
    c_log2 = slopes * LOG2_E
    aux = _alibi_aux(c_log2, ATTN_TK)
    lam_p = lambda_params[0].astype(F32)
    gsub = row(g_subln[0])

    def layer0(h, conv_past, ffn_past, nb, tm, tm_ffn):
        h1, conv_state = _conv_module(h, conv_past, row(g_mix[0]), wpw1, w_dw[0], row(b_dw[0]),
                                      row(ln_g[0]), row(ln_b[0]), wpw2, nb=nb, tm=tm)
        h2, ffn_state = _ffn(h1, ffn_past, row(g_ffn[0]), win[0], w_ffn_dw[0], row(b_ffn_dw[0]), wdown[0],
                             nb=nb, tm=tm_ffn)
        return h2, conv_state, ffn_state

    def project(h2d, tm):
        return _kvq(h2d, row(g_kv), wkv, gk_t, row(g_mix[1]), wq, gq_t, ones_blk, tm=tm)

    def layer1_ffn(h, o, ffn_past, nb, tm):
        return _ffn(h, ffn_past, row(g_ffn[1]), win[1], w_ffn_dw[1], row(b_ffn_dw[1]), wdown[1],
                    nb=nb, tm=tm, o=o, wo=wo)

    h_m = meta_tokens.astype(F32)[None]
    zeros_conv = jnp.zeros((1, CONV_HALO, D_CONV), F32)
    zeros_ffn = jnp.zeros((1, FFN_HALO, D_FF), F32)
    h2_m, conv_m, ffn0_m = layer0(h_m, zeros_conv, zeros_ffn, 1, N_META, N_META)
    k_m, v_m, kh_m, vh_m, qh_m = project(h2_m[0], N_META)
    pad_keys = lambda a: jnp.pad(a, ((0, 0), (0, LANES - N_META), (0, 0)))
    kh_m, vh_m = pad_keys(kh_m), pad_keys(vh_m)
    o_m = _meta_attention(qh_m, kh_m, vh_m, aux, lam_p, gsub)
    _, ffn1_m = layer1_ffn(h2_m, o_m[None], zeros_ffn, 1, N_META)

    h2_p, conv_p, ffn0_p = layer0(x_prompt, conv_m, ffn0_m, 1, ROW_TILE, FFN_ROW_TILE)
    k_p, v_p, kh_p, vh_p, qh_p = project(h2_p[0], ROW_TILE)
    o_p = _prompt_attention(qh_p, kh_p, vh_p, kh_m, vh_m, c_log2, aux, lam_p, gsub, tq=ATTN_TQ, tk=ATTN_TK)
    y_p, ffn1_p = layer1_ffn(h2_p, o_p[None], ffn1_m, 1, FFN_ROW_TILE)

    h2_s, conv_s, ffn0_s = layer0(x_sample, state_conv[0], state_ffn[0], n_s, dec_seq, dec_seq)
    k_s, v_s, _, _, qh_s = project(h2_s.reshape(n_s * dec_seq, D_MODEL), n_s * dec_seq)
    n_rows = 2 * N_HEADS * dec_seq
    q_g = qh_s.reshape(N_HEADS, n_s, 1, dec_seq, V_DIM).transpose(1, 0, 2, 3, 4)
    half = (np.arange(V_DIM) // HEAD_DIM)[None, :] == np.arange(2)[:, None]
    wq_rows = jnp.where(jnp.asarray(half)[None, None, :, None, :], q_g, jnp.zeros((), BF16))
    wq_rows = wq_rows.reshape(n_s, n_rows, V_DIM)
    c_rows = jnp.repeat(c_log2, 2 * dec_seq).reshape(n_rows, 1)
    chunk_cols = PAGES_PER_STEP * PAGE_SIZE * N_HEADS
    col = np.arange(chunk_cols)
    head_match = jnp.asarray((col % N_HEADS)[None, :] == (np.arange(n_rows) // (2 * dec_seq))[:, None])
    bias = jnp.where(head_match, c_rows * jnp.asarray(col // N_HEADS, F32)[None, :], NEG)
    new_rows = lambda a: a.reshape(n_s, dec_seq * N_HEADS, V_DIM)
    o_s = _sample_attention(wq_rows, cache_k, cache_v, page_table, new_rows(k_s), new_rows(v_s),
                            bias, c_rows, lam_p, gsub, pps=PAGES_PER_STEP)
    y_s, ffn1_s = layer1_ffn(h2_s, o_s, state_ffn[1], n_s, dec_seq)

    t_all = N_META + seq
    k_prompt = jnp.concatenate([k_m, k_p], axis=0).reshape(1, t_all, N_HEADS, 2 * HEAD_DIM)
    v_prompt = jnp.concatenate([v_m, v_p], axis=0).reshape(1, t_all, N_HEADS, V_DIM)
    return (y_p, y_s, k_prompt, v_prompt,
            k_s.reshape(n_s, dec_seq, N_HEADS, 2 * HEAD_DIM), v_s.reshape(n_s, dec_seq, N_HEADS, V_DIM),
            conv_p[None], conv_s[None],
            jnp.stack([ffn0_p, ffn1_p]), jnp.stack([ffn0_s, ffn1_s]))
```

```python
import functools
import math

import numpy as np
import jax
import jax.numpy as jnp
from jax import lax
from jax.experimental import pallas as pl
from jax.experimental.pallas import tpu as pltpu

F32 = jnp.float32
BF16 = jnp.bfloat16

D_MODEL = 1024
N_META = 16
CONV_WIDTH = 31
D_CONV = D_MODEL
N_HEADS = 8
HEAD_DIM = 64
V_DIM = 2 * HEAD_DIM
D_QK = N_HEADS * 2 * HEAD_DIM
D_V = N_HEADS * V_DIM
D_FF = ((8 * D_MODEL // 3 + 127) // 128) * 128
FFN_CONV_WIDTH = 3
PAGE_SIZE = 128
EPS = 1e-6
NEG = -1e30
LAMBDA_INIT_L1 = 0.8 - 0.6 * math.exp(-0.3 * 1)
LOG2_E = math.log2(math.e)

LANES = 128
SUBLANES = 8
MXU_DIM = 256
VMEM_LIMIT_BYTES = 56 * 1024 * 1024

CONV_HALO = CONV_WIDTH - 1
CONV_HALO_PAD = 32
FFN_HALO = FFN_CONV_WIDTH - 1
FFN_HALO_PAD = SUBLANES

ROW_TILE = 512
FFN_ROW_TILE = 256
ATTN_TQ = 512
ATTN_TK = 1024
PAGES_PER_STEP = 8
SAMPLE_PAGE_GROUP = 2


def _const_spec(shape):
    zeros = (0,) * len(shape)
    return pl.BlockSpec(shape, lambda *_: zeros, pipeline_mode=pl.Buffered(1))


def _params(n_axes):
    return pltpu.CompilerParams(
        dimension_semantics=("arbitrary",) * n_axes,
        vmem_limit_bytes=VMEM_LIMIT_BYTES,
    )


def _rms_norm(x, g):
    return x * lax.rsqrt(jnp.mean(x * x, axis=-1, keepdims=True) + EPS) * g


def _silu(x):
    return x * jax.nn.sigmoid(x)


def _conv_module_kernel(h_ref, past_ref, g_ref, wpw1_ref, wdw_ref, bdw_ref, lng_ref, lnb_ref, wpw2_ref,
                        out_ref, state_ref, buf_ref, *shift_refs, nb, tm):
    lo = CONV_HALO_PAD - CONV_HALO

    @pl.when(pl.program_id(1) == 0)
    def _load_history():
        buf_ref[:, lo:CONV_HALO_PAD, :] = past_ref[...]

    x = h_ref[...].reshape(nb * tm, D_MODEL)
    u = _rms_norm(x, g_ref[...]).astype(BF16)
    pg = jnp.dot(u, wpw1_ref[...], preferred_element_type=F32)
    glu = pg[:, :D_CONV] * jax.nn.sigmoid(pg[:, D_CONV:])
    buf_ref[:, CONV_HALO_PAD:CONV_HALO_PAD + tm, :] = glu.reshape(nb, tm, D_CONV)

    if shift_refs:
        (shift_ref,) = shift_refs
        for r in range(1, SUBLANES):
            shift_ref[r - 1] = buf_ref[:, r:r + shift_ref.shape[2], :]
    acc = jnp.broadcast_to(bdw_ref[...].reshape(1, 1, D_CONV), (nb, tm, D_CONV))
    for w in range(CONV_WIDTH):
        r = (lo + w) % SUBLANES
        base = lo + w - r
        if shift_refs and r:
            rows = shift_ref[r - 1, :, base:base + tm, :]
        else:
            rows = buf_ref[:, lo + w:lo + w + tm, :]
        acc = acc + wdw_ref[w:w + 1, :].reshape(1, 1, D_CONV) * rows

    new_state = buf_ref[:, tm + lo:tm + CONV_HALO_PAD, :]
    state_ref[...] = new_state
    buf_ref[:, lo:CONV_HALO_PAD, :] = new_state

    c = acc.reshape(nb * tm, D_CONV)
    mu = jnp.mean(c, axis=-1, keepdims=True)
    cc = c - mu
    c = cc * lax.rsqrt(jnp.mean(cc * cc, axis=-1, keepdims=True) + EPS) * lng_ref[...] + lnb_ref[...]
    y = jnp.dot(_silu(c).astype(BF16), wpw2_ref[...], preferred_element_type=F32)
    out_ref[...] = (x + y).reshape(nb, tm, D_MODEL)


def _conv_module(h, past, g, wpw1, wdw, bdw, lng, lnb, wpw2, *, nb, tm):
    ns, length, _ = h.shape
    grid = (ns // nb, length // tm)
    kern = functools.partial(_conv_module_kernel, nb=nb, tm=tm)
    scratch = [pltpu.VMEM((nb, CONV_HALO_PAD + tm, D_CONV), F32)]
    if tm > CONV_HALO_PAD:
        scratch.append(pltpu.VMEM((SUBLANES - 1, nb, CONV_HALO_PAD - SUBLANES + tm, D_CONV), F32))
    return pl.pallas_call(
        kern,
        grid=grid,
        in_specs=[
            pl.BlockSpec((nb, tm, D_MODEL), lambda s, t: (s, t, 0)),
            pl.BlockSpec((nb, CONV_HALO, D_CONV), lambda s, t: (s, 0, 0)),
            _const_spec((1, D_MODEL)),
            _const_spec((D_MODEL, 2 * D_CONV)),
            _const_spec((CONV_WIDTH, D_CONV)),
            _const_spec((1, D_CONV)),
            _const_spec((1, D_CONV)),
            _const_spec((1, D_CONV)),
            _const_spec((D_CONV, D_MODEL)),
        ],
        out_specs=[
            pl.BlockSpec((nb, tm, D_MODEL), lambda s, t: (s, t, 0)),
            pl.BlockSpec((nb, CONV_HALO, D_CONV), lambda s, t: (s, 0, 0)),
        ],
        out_shape=[
            jax.ShapeDtypeStruct((ns, length, D_MODEL), F32),
            jax.ShapeDtypeStruct((ns, CONV_HALO, D_CONV), F32),
        ],
        scratch_shapes=scratch,
        compiler_params=_params(2),
        name="conv_module",
    )(h, past, g, wpw1, wdw, bdw, lng, lnb, wpw2)


def _ffn_kernel(*refs, nb, tm, with_proj):
    if with_proj:
        (h_ref, o_ref, wo_ref, past_ref, g_ref, win_ref, wdw_ref, bdw_ref, wdown_ref,
         out_ref, state_ref, buf_ref) = refs
    else:
        (h_ref, past_ref, g_ref, win_ref, wdw_ref, bdw_ref, wdown_ref,
         out_ref, state_ref, buf_ref) = refs
    lo = FFN_HALO_PAD - FFN_HALO

    @pl.when(pl.program_id(1) == 0)
    def _load_history():
        buf_ref[:, lo:FFN_HALO_PAD, :] = past_ref[...]

    x = h_ref[...].reshape(nb * tm, D_MODEL)
    if with_proj:
        o = o_ref[...].reshape(nb * tm, D_V).astype(BF16)
        x = x + jnp.dot(o, wo_ref[...], preferred_element_type=F32)
    u = _rms_norm(x, g_ref[...]).astype(BF16)
    ag = jnp.dot(u, win_ref[...], preferred_element_type=F32)
    a = ag[:, :D_FF].reshape(nb, tm, D_FF)
    gate = ag[:, D_FF:]
    buf_ref[:, FFN_HALO_PAD:FFN_HALO_PAD + tm, :] = a

    conv = (bdw_ref[...].reshape(1, 1, D_FF)
            + wdw_ref[0:1, :].reshape(1, 1, D_FF) * buf_ref[:, lo:lo + tm, :]
            + wdw_ref[1:2, :].reshape(1, 1, D_FF) * buf_ref[:, lo + 1:lo + 1 + tm, :]
            + wdw_ref[2:3, :].reshape(1, 1, D_FF) * a)

    new_state = buf_ref[:, tm + lo:tm + FFN_HALO_PAD, :]
    state_ref[...] = new_state
    buf_ref[:, lo:FFN_HALO_PAD, :] = new_state

    act = (_silu(conv.reshape(nb * tm, D_FF)) * gate).astype(BF16)
    f = jnp.dot(act, wdown_ref[...], preferred_element_type=F32)
    out_ref[...] = (x + f).reshape(nb, tm, D_MODEL)


def _ffn(h, past, g, win, wdw, bdw, wdown, *, nb, tm, o=None, wo=None):
    ns, length, _ = h.shape
    grid = (ns // nb, length // tm)
    with_proj = o is not None
    kern = functools.partial(_ffn_kernel, nb=nb, tm=tm, with_proj=with_proj)
    row_spec = pl.BlockSpec((nb, tm, D_MODEL), lambda s, t: (s, t, 0))
    state_spec = pl.BlockSpec((nb, FFN_HALO, D_FF), lambda s, t: (s, 0, 0))
    in_specs = [row_spec]
    args = [h]
    if with_proj:
        in_specs += [pl.BlockSpec((nb, tm, D_V), lambda s, t: (s, t, 0)), _const_spec((D_V, D_MODEL))]
        args += [o, wo]
    in_specs += [
        state_spec,
        _const_spec((1, D_MODEL)),
        _const_spec((D_MODEL, 2 * D_FF)),
        _const_spec((FFN_CONV_WIDTH, D_FF)),
        _const_spec((1, D_FF)),
        _const_spec((D_FF, D_MODEL)),
    ]
    args += [past, g, win, wdw, bdw, wdown]
    return pl.pallas_call(
        kern,
        grid=grid,
        in_specs=in_specs,
        out_specs=[row_spec, state_spec],
        out_shape=[
            jax.ShapeDtypeStruct((ns, length, D_MODEL), F32),
            jax.ShapeDtypeStruct((ns, FFN_HALO, D_FF), F32),
        ],
        scratch_shapes=[pltpu.VMEM((nb, FFN_HALO_PAD + tm, D_FF), F32)],
        compiler_params=_params(2),
        name="conv_ffn_proj" if with_proj else "conv_ffn",
    )(*args)


def _group_rms_norm(x, gain, ones_blk):
    x2 = x * x
    hi = x2.astype(BF16)
    lo = (x2 - hi.astype(F32)).astype(BF16)
    parts = []
    for j in range(x.shape[-1] // MXU_DIM):
        sl = slice(j * MXU_DIM, (j + 1) * MXU_DIM)
        parts.append(jnp.dot(hi[:, sl], ones_blk, preferred_element_type=F32)
                     + jnp.dot(lo[:, sl], ones_blk, preferred_element_type=F32))
    ss = jnp.concatenate(parts, axis=1)
    return x * lax.rsqrt(ss * (1.0 / HEAD_DIM) + EPS) * gain


def _kvq_kernel(*refs, tm, n_lead):
    if n_lead:
        (h_ref, gkv_ref, wkv_ref, gk_ref, gmix_ref, wq_ref, gq_ref, ones_ref, klead_ref, vlead_ref,
         k_ref, v_ref, kh_ref, vh_ref, qh_ref, kcarry_ref, vcarry_ref) = refs
    else:
        (h_ref, gkv_ref, wkv_ref, gk_ref, gmix_ref, wq_ref, gq_ref, ones_ref,
         k_ref, v_ref, kh_ref, vh_ref, qh_ref) = refs
    x = h_ref[...]
    ones_blk = ones_ref[...]
    kv = jnp.dot(_rms_norm(x, gkv_ref[...]).astype(BF16), wkv_ref[...], preferred_element_type=F32)
    k = _group_rms_norm(kv[:, :D_QK], gk_ref[...], ones_blk)
    v = kv[:, D_QK:]
    q = jnp.dot(_rms_norm(x, gmix_ref[...]).astype(BF16), wq_ref[...], preferred_element_type=F32)
    q = _group_rms_norm(q, gq_ref[...], ones_blk) * (HEAD_DIM ** -0.5 * LOG2_E)
    if n_lead:
        @pl.when(pl.program_id(0) == 0)
        def _lead():
            kcarry_ref[...] = klead_ref[...]
            vcarry_ref[...] = vlead_ref[...]

        for out_ref, carry_ref, val in ((k_ref, kcarry_ref, k), (v_ref, vcarry_ref, v)):
            out_ref[0:n_lead, :] = carry_ref[...]
            out_ref[n_lead:tm, :] = val[0:tm - n_lead]
            carry_ref[...] = val[tm - n_lead:tm]
    else:
        k_ref[...] = k
        v_ref[...] = v
    kb, vb, qb = k.astype(BF16), v.astype(BF16), q.astype(BF16)
    for h in range(N_HEADS):
        sl = slice(h * V_DIM, (h + 1) * V_DIM)
        kh_ref[h] = kb[:, sl]
        vh_ref[h] = vb[:, sl]
        qh_ref[h] = qb[:, sl]


def _kvq(h2d, gkv, wkv, gk_t, gmix, wq, gq_t, ones_blk, *, tm, lead=None):
    rows = h2d.shape[0]
    n_tiles = rows // tm
    n_lead = 0 if lead is None else lead[0].shape[0]
    assert n_lead % SUBLANES == 0 and n_lead < tm
    tile = (lambda t: jnp.minimum(t, n_tiles - 1)) if n_lead else (lambda t: t)
    row_spec = pl.BlockSpec((tm, D_MODEL), lambda t: (tile(t), 0))
    out_row_spec = pl.BlockSpec((tm, D_MODEL), lambda t: (t, 0))
    head_spec = pl.BlockSpec((N_HEADS, tm, V_DIM), lambda t: (0, tile(t), 0))
    in_specs = [
        row_spec,
        _const_spec((1, D_MODEL)),
        _const_spec((D_MODEL, D_QK + D_V)),
        _const_spec((1, D_QK)),
        _const_spec((1, D_MODEL)),
        _const_spec((D_MODEL, D_QK)),
        _const_spec((1, D_QK)),
        _const_spec((MXU_DIM, MXU_DIM)),
    ]
    args = [h2d, gkv, wkv, gk_t, gmix, wq, gq_t, ones_blk]
    scratch = []
    if n_lead:
        in_specs += [_const_spec((n_lead, D_QK)), _const_spec((n_lead, D_V))]
        args += list(lead)
        scratch = [pltpu.VMEM((n_lead, D_QK), F32), pltpu.VMEM((n_lead, D_V), F32)]
    return pl.pallas_call(
        functools.partial(_kvq_kernel, tm=tm, n_lead=n_lead),
        grid=(n_tiles + (1 if n_lead else 0),),
        in_specs=in_specs,
        out_specs=[out_row_spec, out_row_spec, head_spec, head_spec, head_spec],
        out_shape=[
            jax.ShapeDtypeStruct((n_lead + rows, D_QK), F32),
            jax.ShapeDtypeStruct((n_lead + rows, D_V), F32),
            jax.ShapeDtypeStruct((N_HEADS, rows, V_DIM), BF16),
            jax.ShapeDtypeStruct((N_HEADS, rows, V_DIM), BF16),
            jax.ShapeDtypeStruct((N_HEADS, rows, V_DIM), BF16),
        ],
        scratch_shapes=scratch,
        compiler_params=_params(1),
        name="kvq_proj",
    )(*args)


def _lambda_value(lam_ref):
    lp = lam_ref[...]
    a = jnp.sum(lp[0:1, :] * lp[1:2, :], axis=-1, keepdims=True)
    b = jnp.sum(lp[2:3, :] * lp[3:4, :], axis=-1, keepdims=True)
    return jnp.exp(a) - jnp.exp(b) + LAMBDA_INIT_L1


def _sub_layer_norm(o, gsub):
    o = o * lax.rsqrt(jnp.mean(o * o, axis=-1, keepdims=True) + EPS) * gsub
    return o * (1.0 - LAMBDA_INIT_L1)


def _int_to_f32(x):
    return (x + jnp.zeros((1, 1), jnp.int32)).astype(F32)


def _log2(n):
    assert n > 0 and n & (n - 1) == 0, n
    return n.bit_length() - 1


FIRST, INTERIOR, DIAGONAL = 0, 1, 2
MASKED_POS = 1 << 30


def _stacked_queries(q_ref, h):
    qh = q_ref[h]
    first_map = lax.broadcasted_iota(jnp.int32, (1, V_DIM), 1) < HEAD_DIM
    zero = jnp.zeros_like(qh)
    return jnp.concatenate([jnp.where(first_map, qh, zero), jnp.where(first_map, zero, qh)], axis=0)


def _query_pos(q0, tq, n_keys):
    row = lax.broadcasted_iota(jnp.int32, (2 * tq, n_keys), 0)
    return q0 + (row & (tq - 1))


def _nt_dot(a, b):
    return lax.dot_general(a, b, (((1,), (1,)), ((), ())), preferred_element_type=F32)


def _finalize_heads(acc_ref, lam_ref, gsub_ref, o_ref, tq):
    lam = _lambda_value(lam_ref)
    gsub = gsub_ref[...]
    for h in range(N_HEADS):
        acc = acc_ref[h]
        num = acc[:, :V_DIM] * (1.0 / acc[:, V_DIM:V_DIM + 1])
        o = num[:tq] - lam * num[tq:]
        o_ref[:, h * V_DIM:(h + 1) * V_DIM] = _sub_layer_norm(o, gsub).astype(o_ref.dtype)


def _meta_attn_kernel(q_ref, km_ref, vm_ref, bias_ref, vaux_ref, lam_ref, gsub_ref, o_ref, acc_ref, *, tq):
    def body(h, carry):
        s = _nt_dot(_stacked_queries(q_ref, h), km_ref[h]) + bias_ref[h]
        col = lax.broadcasted_iota(jnp.int32, (1, LANES), 1)
        s = jnp.where(col <= _query_pos(0, tq, LANES), s, NEG)
        p = jnp.exp2(s - jnp.max(s, axis=-1, keepdims=True))
        vv = jnp.concatenate([vm_ref[h], vaux_ref[...]], axis=1)
        acc_ref[h] = jnp.dot(p.astype(BF16), vv, preferred_element_type=F32)
        return carry

    lax.fori_loop(0, N_HEADS, body, 0)
    _finalize_heads(acc_ref, lam_ref, gsub_ref, o_ref, tq)


def _prompt_attn_kernel(qi_tab, ki_tab, kind_tab, last_tab, c_ref, q_ref, k_ref, v_ref, km_ref, vm_ref,
                        biasf_ref, biasm_ref, vaux_ref, lam_ref, gsub_ref, o_ref, m_ref, acc_ref, s0_ref, s1_ref,
                        *, tq, tk):
    step = pl.program_id(0)
    q0 = N_META + qi_tab[step] * tq
    k0 = N_META + ki_tab[step] * tk

    def qk(h, s_ref, kind):
        q2 = _stacked_queries(q_ref, h)
        if kind == FIRST:
            keys = jnp.concatenate([km_ref[h], k_ref[h]], axis=0)
            s = _nt_dot(q2, keys) + biasf_ref[h]
            col = lax.broadcasted_iota(jnp.int32, (1, LANES + tk), 1)
            kpos = jnp.where(col < LANES, jnp.where(col < N_META, col, MASKED_POS), col + (N_META - LANES))
            s_ref[...] = jnp.where(kpos <= _query_pos(q0, tq, LANES + tk), s, NEG)
            return
        s = _nt_dot(q2, k_ref[h]) + biasm_ref[h]
        if kind == DIAGONAL:
            kpos = k0 + lax.broadcasted_iota(jnp.int32, (1, tk), 1)
            s = jnp.where(kpos <= _query_pos(q0, tq, tk), s, NEG)
        s_ref[:, 0:tk] = s

    def softmax_pv(h, s_ref, kind):
        if kind == FIRST:
            s, vals, n_keys, key0 = s_ref[...], jnp.concatenate([vm_ref[h], v_ref[h]], axis=0), LANES + tk, 0
        else:
            s, vals, n_keys, key0 = s_ref[:, 0:tk], v_ref[h], tk, k0
        vv = jnp.concatenate([vals, vaux_ref[0:n_keys, :]], axis=1)
        c = c_ref[h] * _int_to_f32(key0 - q0)
        m_new = jnp.max(s, axis=-1, keepdims=True) + c
        if kind == FIRST:
            p = jnp.exp2(s - (m_new - c))
            acc_ref[h] = jnp.dot(p.astype(BF16), vv, preferred_element_type=F32)
        else:
            m_old = m_ref[h]
            m_new = jnp.maximum(m_old, m_new)
            p = jnp.exp2(s - (m_new - c))
            acc_ref[h] = (jnp.exp2(m_old - m_new) * acc_ref[h]
                          + jnp.dot(p.astype(BF16), vv, preferred_element_type=F32))
        m_ref[h] = m_new

    def run(kind):
        qk(0, s0_ref, kind)

        def body(i, carry):
            a = 2 * i
            qk(a + 1, s1_ref, kind)
            softmax_pv(a, s0_ref, kind)
            qk(a + 2, s0_ref, kind)
            softmax_pv(a + 1, s1_ref, kind)
            return carry

        lax.fori_loop(0, N_HEADS // 2 - 1, body, 0)
        qk(N_HEADS - 1, s1_ref, kind)
        softmax_pv(N_HEADS - 2, s0_ref, kind)
        softmax_pv(N_HEADS - 1, s1_ref, kind)

    for kind in (FIRST, INTERIOR, DIAGONAL):
        pl.when(kind_tab[step] == kind)(functools.partial(run, kind))

    @pl.when(last_tab[step] == 1)
    def _finish():
        _finalize_heads(acc_ref, lam_ref, gsub_ref, o_ref, tq)


def _attn_tables(n_q, tq, tk):
    qi, ki, kind, last = [], [], [], []
    for q in range(n_q):
        n_k = ((q + 1) * tq - 1) // tk + 1
        for k in range(n_k):
            qi.append(q)
            ki.append(k)
            last.append(int(k == n_k - 1))
            kind.append(FIRST if k == 0 else (DIAGONAL if k == n_k - 1 else INTERIOR))
    return tuple(jnp.asarray(np.asarray(t, np.int32)) for t in (qi, ki, kind, last))


def _alibi_tables(c_log2, tk):
    j_tile = np.arange(tk, dtype=np.float32)
    j_first = np.concatenate([np.arange(LANES, dtype=np.float32), N_META + j_tile])
    rows = lambda j: c_log2[:, None, None] * jnp.asarray(j)[None, None, :]
    vaux = np.zeros((LANES + tk, LANES), np.float32)
    vaux[:, 0] = 1.0
    return rows(j_first), rows(j_tile), jnp.asarray(vaux, BF16)


def _meta_attention(qh, kmeta, vmeta, tables, lam_params, gsub):
    tq = qh.shape[1]
    _, biasm, vaux = tables
    return pl.pallas_call(
        functools.partial(_meta_attn_kernel, tq=tq),
        grid=(1,),
        in_specs=[
            _const_spec((N_HEADS, tq, V_DIM)),
            _const_spec((N_HEADS, LANES, V_DIM)),
            _const_spec((N_HEADS, LANES, V_DIM)),
            _const_spec((N_HEADS, 1, LANES)),
            _const_spec((LANES, LANES)),
            _const_spec((4, HEAD_DIM)),
            _const_spec((1, V_DIM)),
        ],
        out_specs=pl.BlockSpec((tq, D_V), lambda i: (0, 0)),
        out_shape=jax.ShapeDtypeStruct((tq, D_V), BF16),
        scratch_shapes=[pltpu.VMEM((N_HEADS, 2 * tq, 2 * V_DIM), F32)],
        compiler_params=_params(1),
        name="meta_attn",
    )(qh, kmeta, vmeta, biasm[:, :, :LANES], vaux[:LANES], lam_params, gsub)


def _prompt_attention(qh, kh, vh, kmeta, vmeta, c_log2, alibi, lam_params, gsub, *, tq, tk):
    t_q = qh.shape[1]
    biasf, biasm, vaux = alibi
    tables = _attn_tables(t_q // tq, tq, tk)
    q_spec = pl.BlockSpec((N_HEADS, tq, V_DIM), lambda s, qi, ki, kd, la: (0, qi[s], 0))
    kv_spec = pl.BlockSpec((N_HEADS, tk, V_DIM), lambda s, qi, ki, kd, la: (0, ki[s], 0))
    grid_spec = pltpu.PrefetchScalarGridSpec(
        num_scalar_prefetch=4,
        grid=(tables[0].shape[0],),
        in_specs=[
            pl.BlockSpec(memory_space=pltpu.SMEM),
            q_spec, kv_spec, kv_spec,
            _const_spec((N_HEADS, LANES, V_DIM)),
            _const_spec((N_HEADS, LANES, V_DIM)),
            _const_spec((N_HEADS, 1, LANES + tk)),
            _const_spec((N_HEADS, 1, tk)),
            _const_spec((LANES + tk, LANES)),
            _const_spec((4, HEAD_DIM)),
            _const_spec((1, V_DIM)),
        ],
        out_specs=pl.BlockSpec((tq, D_V), lambda s, qi, ki, kd, la: (qi[s], 0)),
        scratch_shapes=[
            pltpu.VMEM((N_HEADS, 2 * tq, 1), F32),
            pltpu.VMEM((N_HEADS, 2 * tq, 2 * V_DIM), F32),
            pltpu.VMEM((2 * tq, LANES + tk), F32),
            pltpu.VMEM((2 * tq, LANES + tk), F32),
        ],
    )
    return pl.pallas_call(
        functools.partial(_prompt_attn_kernel, tq=tq, tk=tk),
        grid_spec=grid_spec,
        out_shape=jax.ShapeDtypeStruct((t_q, D_V), BF16),
        compiler_params=_params(1),
        name="prompt_attn",
    )(*tables, c_log2, qh, kh, vh, kmeta, vmeta, biasf, biasm, vaux, lam_params, gsub)


def _sample_attn_kernel(*refs, pps, n_steps, past_len, n_q):
    wq_ref = refs[1]
    k_refs = refs[2:2 + pps]
    v_refs = refs[2 + pps:2 + 2 * pps]
    knew_ref, vnew_ref, bias_ref, c_ref, lam_ref, gsub_ref, o_ref, m_ref, l_ref, acc_ref = refs[2 + 2 * pps:]
    j = pl.program_id(1)
    n_rows = N_HEADS * 2 * n_q
    page_rows = PAGE_SIZE * N_HEADS
    wq = wq_ref[...]
    c_rows = c_ref[...]

    def scores(keys, s_bias):
        return _nt_dot(wq, keys) + s_bias

    def softmax_pv(s, vals, c):
        m_old = m_ref[...]
        m_new = jnp.maximum(m_old, jnp.max(s, axis=-1, keepdims=True) + c)
        alpha = jnp.exp2(m_old - m_new)
        p = jnp.exp2(s - (m_new - c))
        l_ref[...] = alpha * l_ref[...] + jnp.sum(p, axis=-1, keepdims=True)
        acc_ref[...] = alpha * acc_ref[...] + jnp.dot(p.astype(BF16), vals, preferred_element_type=F32)
        m_ref[...] = m_new

    @pl.when(j == 0)
    def _init():
        m_ref[...] = jnp.full(m_ref.shape, NEG, F32)
        l_ref[...] = jnp.zeros(l_ref.shape, F32)
        acc_ref[...] = jnp.zeros(acc_ref.shape, F32)

    as_rows = lambda r: r[...].reshape(page_rows, V_DIM).astype(BF16)
    c_step = c_rows * _int_to_f32(j * (pps * PAGE_SIZE) - past_len)
    def group_scores(g):
        lo = g * SAMPLE_PAGE_GROUP
        keys = jnp.concatenate([as_rows(r) for r in k_refs[lo:lo + SAMPLE_PAGE_GROUP]], axis=0)
        return scores(keys, bias_ref[:, lo * page_rows:(lo + SAMPLE_PAGE_GROUP) * page_rows])

    n_groups = pps // SAMPLE_PAGE_GROUP
    s_next = group_scores(0)
    for g in range(n_groups):
        s_cur = s_next
        if g + 1 < n_groups:
            s_next = group_scores(g + 1)
        lo = g * SAMPLE_PAGE_GROUP
        vals = jnp.concatenate([as_rows(r) for r in v_refs[lo:lo + SAMPLE_PAGE_GROUP]], axis=0)
        softmax_pv(s_cur, vals, c_step)

    @pl.when(j == n_steps - 1)
    def _finish():
        n_new = n_q * N_HEADS
        col = lax.broadcasted_iota(jnp.int32, (n_rows, n_new), 1)
        row = lax.broadcasted_iota(jnp.int32, (n_rows, n_new), 0)
        key = col >> _log2(N_HEADS)
        valid = ((col & (N_HEADS - 1)) == (row >> _log2(2 * n_q))) & (key <= (row & (n_q - 1)))
        s_bias = jnp.where(valid, c_rows * key.astype(F32), NEG)
        softmax_pv(scores(knew_ref[...].astype(BF16), s_bias), vnew_ref[...].astype(BF16), jnp.zeros((1, 1), F32))

        lam = _lambda_value(lam_ref)
        gsub = gsub_ref[...]
        acc = acc_ref[...] * (1.0 / l_ref[...])
        for h in range(N_HEADS):
            r0 = h * 2 * n_q
            o = acc[r0:r0 + n_q] - lam * acc[r0 + n_q:r0 + 2 * n_q]
            o_ref[:, h * V_DIM:(h + 1) * V_DIM] = _sub_layer_norm(o, gsub)


def _sample_attention(wq, cache_k, cache_v, page_table, knew, vnew, bias, c_rows, lam_params, gsub, *, pps):
    n_seq, n_rows, _ = wq.shape
    n_pages = page_table.shape[1]
    n_q = n_rows // (2 * N_HEADS)
    n_steps = n_pages // pps
    kern = functools.partial(_sample_attn_kernel, pps=pps, n_steps=n_steps,
                             past_len=n_pages * PAGE_SIZE, n_q=n_q)

    def page_spec(r):
        return pl.BlockSpec((None, PAGE_SIZE, N_HEADS, V_DIM),
                            lambda n, j, pt: (pt[n * n_pages + j * pps + r], 0, 0, 0))

    seq_spec = lambda rows, cols: pl.BlockSpec((None, rows, cols), lambda n, j, pt: (n, 0, 0))
    grid_spec = pltpu.PrefetchScalarGridSpec(
        num_scalar_prefetch=1,
        grid=(n_seq, n_steps),
        in_specs=([seq_spec(n_rows, V_DIM)]
                  + [page_spec(r) for r in range(pps)]
                  + [page_spec(r) for r in range(pps)]
                  + [seq_spec(n_q * N_HEADS, V_DIM), seq_spec(n_q * N_HEADS, V_DIM),
                     _const_spec(bias.shape), _const_spec((n_rows, 1)),
                     _const_spec((4, HEAD_DIM)), _const_spec((1, V_DIM))]),
        out_specs=seq_spec(n_q, D_V),
        scratch_shapes=[
            pltpu.VMEM((n_rows, 1), F32),
            pltpu.VMEM((n_rows, 1), F32),
            pltpu.VMEM((n_rows, V_DIM), F32),
        ],
    )
    return pl.pallas_call(
        kern,
        grid_spec=grid_spec,
        out_shape=jax.ShapeDtypeStruct((n_seq, n_q, D_V), F32),
        compiler_params=_params(2),
        name="sample_attn",
    )(page_table.reshape(-1), wq, *([cache_k] * pps), *([cache_v] * pps),
      knew, vnew, bias, c_rows, lam_params, gsub)


def kernel(x_prompt, x_sample, cache_k, cache_v, state_conv, state_ffn, page_table, meta_tokens, g_mix, g_ffn,
           w_pw1, w_dw, b_dw, ln_g, ln_b, w_pw2, g_kv, w_kv, g_k, w_q, g_q, lambda_params, g_subln, w_o,
           w_ffn_in, w_ffn_dw, b_ffn_dw, w_ffn_down):
    n_p, seq, _ = x_prompt.shape
    n_s, dec_seq, _ = x_sample.shape
    assert n_p == 1 and seq % ROW_TILE == 0 and seq % ATTN_TK == 0 and dec_seq == SUBLANES
    assert cache_k.shape[1:] == (PAGE_SIZE, N_HEADS, 2 * HEAD_DIM) and cache_v.shape[1:] == (PAGE_SIZE, N_HEADS, V_DIM)

    row = lambda a: a.reshape(1, -1).astype(F32)
    bf = lambda a: a.astype(BF16)
    wpw1, wpw2, wkv, wq, wo = bf(w_pw1[0]), bf(w_pw2[0]), bf(w_kv), bf(w_q[0]), bf(w_o[0])
    win, wdown = bf(w_ffn_in), bf(w_ffn_down)
    gk_t = jnp.tile(g_k.reshape(-1), N_HEADS).reshape(1, D_QK)
    gq_t = jnp.tile(g_q[0].reshape(-1), N_HEADS).reshape(1, D_QK)
    grp = np.arange(MXU_DIM) // HEAD_DIM
    ones_blk = jnp.asarray(grp[:, None] == grp[None, :], BF16)
    slopes = jnp.exp2(-8.0 * jnp.arange(1, N_HEADS + 1, dtype=F32) / N_HEADS)
    c_log2 = slopes * LOG2_E
    alibi = _alibi_tables(c_log2, ATTN_TK)
    lam_p = lambda_params[0].astype(F32)
    gsub = row(g_subln[0])

    def layer0(h, conv_past, ffn_past, nb, tm, tm_ffn):
        h1, conv_state = _conv_module(h, conv_past, row(g_mix[0]), wpw1, w_dw[0], row(b_dw[0]),
                                      row(ln_g[0]), row(ln_b[0]), wpw2, nb=nb, tm=tm)
        h2, ffn_state = _ffn(h1, ffn_past, row(g_ffn[0]), win[0], w_ffn_dw[0], row(b_ffn_dw[0]), wdown[0],
                             nb=nb, tm=tm_ffn)
        return h2, conv_state, ffn_state

    def project(h2d, tm, lead=None):
        return _kvq(h2d, row(g_kv), wkv, gk_t, row(g_mix[1]), wq, gq_t, ones_blk, tm=tm, lead=lead)

    def layer1_ffn(h, o, ffn_past, nb, tm):
        return _ffn(h, ffn_past, row(g_ffn[1]), win[1], w_ffn_dw[1], row(b_ffn_dw[1]), wdown[1],
                    nb=nb, tm=tm, o=o, wo=wo)

    h_m = meta_tokens.astype(F32)[None]
    zeros_conv = jnp.zeros((1, CONV_HALO, D_CONV), F32)
    zeros_ffn = jnp.zeros((1, FFN_HALO, D_FF), F32)
    h2_m, conv_m, ffn0_m = layer0(h_m, zeros_conv, zeros_ffn, 1, N_META, N_META)
    k_m, v_m, kh_m, vh_m, qh_m = project(h2_m[0], N_META)
    pad_keys = lambda a: jnp.pad(a, ((0, 0), (0, LANES - N_META), (0, 0)))
    kh_m, vh_m = pad_keys(kh_m), pad_keys(vh_m)
    o_m = _meta_attention(qh_m, kh_m, vh_m, alibi, lam_p, gsub)
    _, ffn1_m = layer1_ffn(h2_m, o_m[None], zeros_ffn, 1, N_META)

    h2_p, conv_p, ffn0_p = layer0(x_prompt, conv_m, ffn0_m, 1, ROW_TILE, FFN_ROW_TILE)
    k_p, v_p, kh_p, vh_p, qh_p = project(h2_p[0], ROW_TILE, lead=(k_m, v_m))
    o_p = _prompt_attention(qh_p, kh_p, vh_p, kh_m, vh_m, c_log2, alibi, lam_p, gsub, tq=ATTN_TQ, tk=ATTN_TK)
    y_p, ffn1_p = layer1_ffn(h2_p, o_p[None], ffn1_m, 1, FFN_ROW_TILE)

    h2_s, conv_s, ffn0_s = layer0(x_sample, state_conv[0], state_ffn[0], n_s, dec_seq, dec_seq)
    k_s, v_s, _, _, qh_s = project(h2_s.reshape(n_s * dec_seq, D_MODEL), n_s * dec_seq)
    n_rows = 2 * N_HEADS * dec_seq
    q_g = qh_s.reshape(N_HEADS, n_s, 1, dec_seq, V_DIM).transpose(1, 0, 2, 3, 4)
    half = (np.arange(V_DIM) // HEAD_DIM)[None, :] == np.arange(2)[:, None]
    wq_rows = jnp.where(jnp.asarray(half)[None, None, :, None, :], q_g, jnp.zeros((), BF16))
    wq_rows = wq_rows.reshape(n_s, n_rows, V_DIM)
    c_rows = jnp.repeat(c_log2, 2 * dec_seq).reshape(n_rows, 1)
    chunk_cols = PAGES_PER_STEP * PAGE_SIZE * N_HEADS
    col = np.arange(chunk_cols)
    head_match = jnp.asarray((col % N_HEADS)[None, :] == (np.arange(n_rows) // (2 * dec_seq))[:, None])
    bias = jnp.where(head_match, c_rows * jnp.asarray(col // N_HEADS, F32)[None, :], NEG)
    new_rows = lambda a: a.reshape(n_s, dec_seq * N_HEADS, V_DIM)
    o_s = _sample_attention(wq_rows, cache_k, cache_v, page_table, new_rows(k_s), new_rows(v_s),
                            bias, c_rows, lam_p, gsub, pps=PAGES_PER_STEP)
    y_s, ffn1_s = layer1_ffn(h2_s, o_s, state_ffn[1], n_s, dec_seq)

    t_all = N_META + seq
    k_prompt = k_p.reshape(1, t_all, N_HEADS, 2 * HEAD_DIM)
    v_prompt = v_p.reshape(1, t_all, N_HEADS, V_DIM)
    return (y_p, y_s, k_prompt, v_prompt,
            k_s.reshape(n_s, dec_seq, N_HEADS, 2 * HEAD_DIM), v_s.reshape(n_s, dec_seq, N_HEADS, V_DIM),
            conv_p[None], conv_s[None],
            jnp.stack([ffn0_p, ffn1_p]), jnp.stack([ffn0_s, ffn1_s]))
```

```python
import functools
import math

import numpy as np
import jax
import jax.numpy as jnp
from jax import lax
from jax.experimental import pallas as pl
from jax.experimental.pallas import tpu as pltpu

F32 = jnp.float32
BF16 = jnp.bfloat16

D_MODEL = 1024
N_META = 16
CONV_WIDTH = 31
D_CONV = D_MODEL
N_HEADS = 8
HEAD_DIM = 64
V_DIM = 2 * HEAD_DIM
D_QK = N_HEADS * 2 * HEAD_DIM
D_V = N_HEADS * V_DIM
D_FF = ((8 * D_MODEL // 3 + 127) // 128) * 128
FFN_CONV_WIDTH = 3
PAGE_SIZE = 128
EPS = 1e-6
NEG = -1e30
LAMBDA_INIT_L1 = 0.8 - 0.6 * math.exp(-0.3 * 1)
LOG2_E = math.log2(math.e)

LANES = 128
SUBLANES = 8
MXU_DIM = 256
VMEM_LIMIT_BYTES = 56 * 1024 * 1024

CONV_HALO = CONV_WIDTH - 1
CONV_HALO_PAD = 32
FFN_HALO = FFN_CONV_WIDTH - 1
FFN_HALO_PAD = SUBLANES

ROW_TILE = 512
FFN_ROW_TILE = 512
ATTN_TQ = 512
ATTN_TK = 1024
PAGES_PER_STEP = 8
SAMPLE_PAGE_GROUP = 2


def _const_spec(shape):
    zeros = (0,) * len(shape)
    return pl.BlockSpec(shape, lambda *_: zeros, pipeline_mode=pl.Buffered(1))


def _params(n_axes):
    return pltpu.CompilerParams(
        dimension_semantics=("arbitrary",) * n_axes,
        vmem_limit_bytes=VMEM_LIMIT_BYTES,
    )


def _rms_norm(x, g):
    return x * lax.rsqrt(jnp.mean(x * x, axis=-1, keepdims=True) + EPS) * g


def _silu(x):
    return x * jax.nn.sigmoid(x)


def _conv_module_kernel(h_ref, past_ref, g_ref, wpw1_ref, wdw_ref, bdw_ref, lng_ref, lnb_ref, wpw2_ref,
                        out_ref, state_ref, buf_ref, *shift_refs, nb, tm):
    lo = CONV_HALO_PAD - CONV_HALO

    @pl.when(pl.program_id(1) == 0)
    def _load_history():
        buf_ref[:, lo:CONV_HALO_PAD, :] = past_ref[...]

    x = h_ref[...].reshape(nb * tm, D_MODEL)
    u = _rms_norm(x, g_ref[...]).astype(BF16)
    pg = jnp.dot(u, wpw1_ref[...], preferred_element_type=F32)
    glu = pg[:, :D_CONV] * jax.nn.sigmoid(pg[:, D_CONV:])
    buf_ref[:, CONV_HALO_PAD:CONV_HALO_PAD + tm, :] = glu.reshape(nb, tm, D_CONV)

    if shift_refs:
        (shift_ref,) = shift_refs
        for r in range(1, SUBLANES):
            shift_ref[r - 1] = buf_ref[:, r:r + shift_ref.shape[2], :]
    acc = jnp.broadcast_to(bdw_ref[...].reshape(1, 1, D_CONV), (nb, tm, D_CONV))
    for w in range(CONV_WIDTH):
        r = (lo + w) % SUBLANES
        base = lo + w - r
        if shift_refs and r:
            rows = shift_ref[r - 1, :, base:base + tm, :]
        else:
            rows = buf_ref[:, lo + w:lo + w + tm, :]
        acc = acc + wdw_ref[w:w + 1, :].reshape(1, 1, D_CONV) * rows

    new_state = buf_ref[:, tm + lo:tm + CONV_HALO_PAD, :]
    state_ref[...] = new_state
    buf_ref[:, lo:CONV_HALO_PAD, :] = new_state

    c = acc.reshape(nb * tm, D_CONV)
    mu = jnp.mean(c, axis=-1, keepdims=True)
    cc = c - mu
    c = cc * lax.rsqrt(jnp.mean(cc * cc, axis=-1, keepdims=True) + EPS) * lng_ref[...] + lnb_ref[...]
    y = jnp.dot(_silu(c).astype(BF16), wpw2_ref[...], preferred_element_type=F32)
    out_ref[...] = (x + y).reshape(nb, tm, D_MODEL)


def _conv_module(h, past, g, wpw1, wdw, bdw, lng, lnb, wpw2, *, nb, tm):
    ns, length, _ = h.shape
    grid = (ns // nb, length // tm)
    kern = functools.partial(_conv_module_kernel, nb=nb, tm=tm)
    scratch = [pltpu.VMEM((nb, CONV_HALO_PAD + tm, D_CONV), F32)]
    if tm > CONV_HALO_PAD:
        scratch.append(pltpu.VMEM((SUBLANES - 1, nb, CONV_HALO_PAD - SUBLANES + tm, D_CONV), F32))
    return pl.pallas_call(
        kern,
        grid=grid,
        in_specs=[
            pl.BlockSpec((nb, tm, D_MODEL), lambda s, t: (s, t, 0)),
            pl.BlockSpec((nb, CONV_HALO, D_CONV), lambda s, t: (s, 0, 0)),
            _const_spec((1, D_MODEL)),
            _const_spec((D_MODEL, 2 * D_CONV)),
            _const_spec((CONV_WIDTH, D_CONV)),
            _const_spec((1, D_CONV)),
            _const_spec((1, D_CONV)),
            _const_spec((1, D_CONV)),
            _const_spec((D_CONV, D_MODEL)),
        ],
        out_specs=[
            pl.BlockSpec((nb, tm, D_MODEL), lambda s, t: (s, t, 0)),
            pl.BlockSpec((nb, CONV_HALO, D_CONV), lambda s, t: (s, 0, 0)),
        ],
        out_shape=[
            jax.ShapeDtypeStruct((ns, length, D_MODEL), F32),
            jax.ShapeDtypeStruct((ns, CONV_HALO, D_CONV), F32),
        ],
        scratch_shapes=scratch,
        compiler_params=_params(2),
        name="conv_module",
    )(h, past, g, wpw1, wdw, bdw, lng, lnb, wpw2)


def _ffn_kernel(*refs, nb, tm, with_proj):
    if with_proj:
        (h_ref, o_ref, wo_ref, past_ref, g_ref, win_ref, wdw_ref, bdw_ref, wdown_ref,
         out_ref, state_ref, buf_ref) = refs
    else:
        (h_ref, past_ref, g_ref, win_ref, wdw_ref, bdw_ref, wdown_ref,
         out_ref, state_ref, buf_ref) = refs
    lo = FFN_HALO_PAD - FFN_HALO

    @pl.when(pl.program_id(1) == 0)
    def _load_history():
        buf_ref[:, lo:FFN_HALO_PAD, :] = past_ref[...]

    x = h_ref[...].reshape(nb * tm, D_MODEL)
    if with_proj:
        o = o_ref[...].reshape(nb * tm, D_V).astype(BF16)
        x = x + jnp.dot(o, wo_ref[...], preferred_element_type=F32)
    u = _rms_norm(x, g_ref[...]).astype(BF16)
    ag = jnp.dot(u, win_ref[...], preferred_element_type=F32)
    a = ag[:, :D_FF].reshape(nb, tm, D_FF)
    gate = ag[:, D_FF:]
    buf_ref[:, FFN_HALO_PAD:FFN_HALO_PAD + tm, :] = a

    conv = (bdw_ref[...].reshape(1, 1, D_FF)
            + wdw_ref[0:1, :].reshape(1, 1, D_FF) * buf_ref[:, lo:lo + tm, :]
            + wdw_ref[1:2, :].reshape(1, 1, D_FF) * buf_ref[:, lo + 1:lo + 1 + tm, :]
            + wdw_ref[2:3, :].reshape(1, 1, D_FF) * a)

    new_state = buf_ref[:, tm + lo:tm + FFN_HALO_PAD, :]
    state_ref[...] = new_state
    buf_ref[:, lo:FFN_HALO_PAD, :] = new_state

    act = (_silu(conv.reshape(nb * tm, D_FF)) * gate).astype(BF16)
    f = jnp.dot(act, wdown_ref[...], preferred_element_type=F32)
    out_ref[...] = (x + f).reshape(nb, tm, D_MODEL)


def _ffn(h, past, g, win, wdw, bdw, wdown, *, nb, tm, o=None, wo=None):
    ns, length, _ = h.shape
    grid = (ns // nb, length // tm)
    with_proj = o is not None
    kern = functools.partial(_ffn_kernel, nb=nb, tm=tm, with_proj=with_proj)
    row_spec = pl.BlockSpec((nb, tm, D_MODEL), lambda s, t: (s, t, 0))
    state_spec = pl.BlockSpec((nb, FFN_HALO, D_FF), lambda s, t: (s, 0, 0))
    in_specs = [row_spec]
    args = [h]
    if with_proj:
        in_specs += [pl.BlockSpec((nb, tm, D_V), lambda s, t: (s, t, 0)), _const_spec((D_V, D_MODEL))]
        args += [o, wo]
    in_specs += [
        state_spec,
        _const_spec((1, D_MODEL)),
        _const_spec((D_MODEL, 2 * D_FF)),
        _const_spec((FFN_CONV_WIDTH, D_FF)),
        _const_spec((1, D_FF)),
        _const_spec((D_FF, D_MODEL)),
    ]
    args += [past, g, win, wdw, bdw, wdown]
    return pl.pallas_call(
        kern,
        grid=grid,
        in_specs=in_specs,
        out_specs=[row_spec, state_spec],
        out_shape=[
            jax.ShapeDtypeStruct((ns, length, D_MODEL), F32),
            jax.ShapeDtypeStruct((ns, FFN_HALO, D_FF), F32),
        ],
        scratch_shapes=[pltpu.VMEM((nb, FFN_HALO_PAD + tm, D_FF), F32)],
        compiler_params=_params(2),
        name="conv_ffn_proj" if with_proj else "conv_ffn",
    )(*args)


def _group_rms_norm(x, gain, ones_blk):
    x2 = x * x
    hi = x2.astype(BF16)
    lo = (x2 - hi.astype(F32)).astype(BF16)
    parts = []
    for j in range(x.shape[-1] // MXU_DIM):
        sl = slice(j * MXU_DIM, (j + 1) * MXU_DIM)
        parts.append(jnp.dot(hi[:, sl], ones_blk, preferred_element_type=F32)
                     + jnp.dot(lo[:, sl], ones_blk, preferred_element_type=F32))
    ss = jnp.concatenate(parts, axis=1)
    return x * lax.rsqrt(ss * (1.0 / HEAD_DIM) + EPS) * gain


def _kvq_kernel(*refs, tm, n_lead):
    if n_lead:
        (h_ref, gkv_ref, wkv_ref, gk_ref, gmix_ref, wq_ref, gq_ref, ones_ref, klead_ref, vlead_ref,
         k_ref, v_ref, kh_ref, vh_ref, qh_ref, kcarry_ref, vcarry_ref) = refs
    else:
        (h_ref, gkv_ref, wkv_ref, gk_ref, gmix_ref, wq_ref, gq_ref, ones_ref,
         k_ref, v_ref, kh_ref, vh_ref, qh_ref) = refs
    x = h_ref[...]
    ones_blk = ones_ref[...]
    kv = jnp.dot(_rms_norm(x, gkv_ref[...]).astype(BF16), wkv_ref[...], preferred_element_type=F32)
    k = _group_rms_norm(kv[:, :D_QK], gk_ref[...], ones_blk)
    v = kv[:, D_QK:]
    q = jnp.dot(_rms_norm(x, gmix_ref[...]).astype(BF16), wq_ref[...], preferred_element_type=F32)
    q = _group_rms_norm(q, gq_ref[...], ones_blk) * (HEAD_DIM ** -0.5 * LOG2_E)
    if n_lead:
        @pl.when(pl.program_id(0) == 0)
        def _lead():
            kcarry_ref[...] = klead_ref[...]
            vcarry_ref[...] = vlead_ref[...]

        for out_ref, carry_ref, val in ((k_ref, kcarry_ref, k), (v_ref, vcarry_ref, v)):
            out_ref[0:n_lead, :] = carry_ref[...]
            out_ref[n_lead:tm, :] = val[0:tm - n_lead]
            carry_ref[...] = val[tm - n_lead:tm]
    else:
        k_ref[...] = k
        v_ref[...] = v
    kb, vb, qb = k.astype(BF16), v.astype(BF16), q.astype(BF16)
    for h in range(N_HEADS):
        sl = slice(h * V_DIM, (h + 1) * V_DIM)
        kh_ref[h] = kb[:, sl]
        vh_ref[h] = vb[:, sl]
        qh_ref[h] = qb[:, sl]


def _kvq(h2d, gkv, wkv, gk_t, gmix, wq, gq_t, ones_blk, *, tm, lead=None):
    rows = h2d.shape[0]
    n_tiles = rows // tm
    n_lead = 0 if lead is None else lead[0].shape[0]
    assert n_lead % SUBLANES == 0 and n_lead < tm
    tile = (lambda t: jnp.minimum(t, n_tiles - 1)) if n_lead else (lambda t: t)
    row_spec = pl.BlockSpec((tm, D_MODEL), lambda t: (tile(t), 0))
    out_row_spec = pl.BlockSpec((tm, D_MODEL), lambda t: (t, 0))
    head_spec = pl.BlockSpec((N_HEADS, tm, V_DIM), lambda t: (0, tile(t), 0))
    in_specs = [
        row_spec,
        _const_spec((1, D_MODEL)),
        _const_spec((D_MODEL, D_QK + D_V)),
        _const_spec((1, D_QK)),
        _const_spec((1, D_MODEL)),
        _const_spec((D_MODEL, D_QK)),
        _const_spec((1, D_QK)),
        _const_spec((MXU_DIM, MXU_DIM)),
    ]
    args = [h2d, gkv, wkv, gk_t, gmix, wq, gq_t, ones_blk]
    scratch = []
    if n_lead:
        in_specs += [_const_spec((n_lead, D_QK)), _const_spec((n_lead, D_V))]
        args += list(lead)
        scratch = [pltpu.VMEM((n_lead, D_QK), F32), pltpu.VMEM((n_lead, D_V), F32)]
    return pl.pallas_call(
        functools.partial(_kvq_kernel, tm=tm, n_lead=n_lead),
        grid=(n_tiles + (1 if n_lead else 0),),
        in_specs=in_specs,
        out_specs=[out_row_spec, out_row_spec, head_spec, head_spec, head_spec],
        out_shape=[
            jax.ShapeDtypeStruct((n_lead + rows, D_QK), F32),
            jax.ShapeDtypeStruct((n_lead + rows, D_V), F32),
            jax.ShapeDtypeStruct((N_HEADS, rows, V_DIM), BF16),
            jax.ShapeDtypeStruct((N_HEADS, rows, V_DIM), BF16),
            jax.ShapeDtypeStruct((N_HEADS, rows, V_DIM), BF16),
        ],
        scratch_shapes=scratch,
        compiler_params=_params(1),
        name="kvq_proj",
    )(*args)


def _lambda_value(lam_ref):
    lp = lam_ref[...]
    a = jnp.sum(lp[0:1, :] * lp[1:2, :], axis=-1, keepdims=True)
    b = jnp.sum(lp[2:3, :] * lp[3:4, :], axis=-1, keepdims=True)
    return jnp.exp(a) - jnp.exp(b) + LAMBDA_INIT_L1


def _sub_layer_norm(o, gsub):
    o = o * lax.rsqrt(jnp.mean(o * o, axis=-1, keepdims=True) + EPS) * gsub
    return o * (1.0 - LAMBDA_INIT_L1)


def _int_to_f32(x):
    return (x + jnp.zeros((1, 1), jnp.int32)).astype(F32)


def _log2(n):
    assert n > 0 and n & (n - 1) == 0, n
    return n.bit_length() - 1


FIRST, INTERIOR, DIAGONAL = 0, 1, 2
MASKED_POS = 1 << 30


def _stacked_queries(q_ref, h):
    qh = q_ref[h]
    first_map = lax.broadcasted_iota(jnp.int32, (1, V_DIM), 1) < HEAD_DIM
    zero = jnp.zeros_like(qh)
    return jnp.concatenate([jnp.where(first_map, qh, zero), jnp.where(first_map, zero, qh)], axis=0)


def _query_pos(q0, tq, n_keys):
    row = lax.broadcasted_iota(jnp.int32, (2 * tq, n_keys), 0)
    return q0 + (row & (tq - 1))


def _nt_dot(a, b):
    return lax.dot_general(a, b, (((1,), (1,)), ((), ())), preferred_element_type=F32)


def _finalize_heads(acc_ref, lam_ref, gsub_ref, o_ref, tq):
    lam = _lambda_value(lam_ref)
    gsub = gsub_ref[...]
    for h in range(N_HEADS):
        acc = acc_ref[h]
        num = acc[:, :V_DIM] * (1.0 / acc[:, V_DIM:V_DIM + 1])
        o = num[:tq] - lam * num[tq:]
        o_ref[:, h * V_DIM:(h + 1) * V_DIM] = _sub_layer_norm(o, gsub).astype(o_ref.dtype)


def _meta_attn_kernel(q_ref, km_ref, vm_ref, bias_ref, vaux_ref, lam_ref, gsub_ref, o_ref, acc_ref, *, tq):
    def body(h, carry):
        s = _nt_dot(_stacked_queries(q_ref, h), km_ref[h]) + bias_ref[h]
        col = lax.broadcasted_iota(jnp.int32, (1, LANES), 1)
        s = jnp.where(col <= _query_pos(0, tq, LANES), s, NEG)
        p = jnp.exp2(s - jnp.max(s, axis=-1, keepdims=True))
        vv = jnp.concatenate([vm_ref[h], vaux_ref[...]], axis=1)
        acc_ref[h] = jnp.dot(p.astype(BF16), vv, preferred_element_type=F32)
        return carry

    lax.fori_loop(0, N_HEADS, body, 0)
    _finalize_heads(acc_ref, lam_ref, gsub_ref, o_ref, tq)


def _prompt_attn_kernel(qi_tab, ki_tab, kind_tab, last_tab, c_ref, q_ref, k_ref, v_ref, km_ref, vm_ref,
                        biasf_ref, biasm_ref, vaux_ref, lam_ref, gsub_ref, o_ref, m_ref, acc_ref, s0_ref, s1_ref,
                        *, tq, tk):
    step = pl.program_id(0)
    q0 = N_META + qi_tab[step] * tq
    k0 = N_META + ki_tab[step] * tk

    def qk(h, s_ref, kind):
        q2 = _stacked_queries(q_ref, h)
        if kind == FIRST:
            keys = jnp.concatenate([km_ref[h], k_ref[h]], axis=0)
            s = _nt_dot(q2, keys) + biasf_ref[h]
            col = lax.broadcasted_iota(jnp.int32, (1, LANES + tk), 1)
            kpos = jnp.where(col < LANES, jnp.where(col < N_META, col, MASKED_POS), col + (N_META - LANES))
            s_ref[...] = jnp.where(kpos <= _query_pos(q0, tq, LANES + tk), s, NEG)
            return
        s = _nt_dot(q2, k_ref[h]) + biasm_ref[h]
        if kind == DIAGONAL:
            kpos = k0 + lax.broadcasted_iota(jnp.int32, (1, tk), 1)
            s = jnp.where(kpos <= _query_pos(q0, tq, tk), s, NEG)
        s_ref[:, 0:tk] = s

    def softmax_pv(h, s_ref, kind):
        if kind == FIRST:
            s, vals, n_keys, key0 = s_ref[...], jnp.concatenate([vm_ref[h], v_ref[h]], axis=0), LANES + tk, 0
        else:
            s, vals, n_keys, key0 = s_ref[:, 0:tk], v_ref[h], tk, k0
        vv = jnp.concatenate([vals, vaux_ref[0:n_keys, :]], axis=1)
        c = c_ref[h] * _int_to_f32(key0 - q0)
        m_new = jnp.max(s, axis=-1, keepdims=True) + c
        if kind == FIRST:
            p = jnp.exp2(s - (m_new - c))
            acc_ref[h] = jnp.dot(p.astype(BF16), vv, preferred_element_type=F32)
        else:
            m_old = m_ref[h]
            m_new = jnp.maximum(m_old, m_new)
            p = jnp.exp2(s - (m_new - c))
            acc_ref[h] = (jnp.exp2(m_old - m_new) * acc_ref[h]
                          + jnp.dot(p.astype(BF16), vv, preferred_element_type=F32))
        m_ref[h] = m_new

    def run(kind):
        s_refs = (s0_ref, s1_ref)
        qk(0, s0_ref, kind)
        for h in range(N_HEADS):
            if h + 1 < N_HEADS:
                qk(h + 1, s_refs[(h + 1) % 2], kind)
            softmax_pv(h, s_refs[h % 2], kind)

    for kind in (FIRST, INTERIOR, DIAGONAL):
        pl.when(kind_tab[step] == kind)(functools.partial(run, kind))

    @pl.when(last_tab[step] == 1)
    def _finish():
        _finalize_heads(acc_ref, lam_ref, gsub_ref, o_ref, tq)


def _attn_tables(n_q, tq, tk):
    qi, ki, kind, last = [], [], [], []
    for q in range(n_q):
        n_k = ((q + 1) * tq - 1) // tk + 1
        for k in range(n_k):
            qi.append(q)
            ki.append(k)
            last.append(int(k == n_k - 1))
            kind.append(FIRST if k == 0 else (DIAGONAL if k == n_k - 1 else INTERIOR))
    return tuple(jnp.asarray(np.asarray(t, np.int32)) for t in (qi, ki, kind, last))


def _alibi_tables(c_log2, tk):
    j_tile = np.arange(tk, dtype=np.float32)
    j_first = np.concatenate([np.arange(LANES, dtype=np.float32), N_META + j_tile])
    rows = lambda j: c_log2[:, None, None] * jnp.asarray(j)[None, None, :]
    vaux = np.zeros((LANES + tk, LANES), np.float32)
    vaux[:, 0] = 1.0
    return rows(j_first), rows(j_tile), jnp.asarray(vaux, BF16)


def _meta_attention(qh, kmeta, vmeta, tables, lam_params, gsub):
    tq = qh.shape[1]
    _, biasm, vaux = tables
    return pl.pallas_call(
        functools.partial(_meta_attn_kernel, tq=tq),
        grid=(1,),
        in_specs=[
            _const_spec((N_HEADS, tq, V_DIM)),
            _const_spec((N_HEADS, LANES, V_DIM)),
            _const_spec((N_HEADS, LANES, V_DIM)),
            _const_spec((N_HEADS, 1, LANES)),
            _const_spec((LANES, LANES)),
            _const_spec((4, HEAD_DIM)),
            _const_spec((1, V_DIM)),
        ],
        out_specs=pl.BlockSpec((tq, D_V), lambda i: (0, 0)),
        out_shape=jax.ShapeDtypeStruct((tq, D_V), BF16),
        scratch_shapes=[pltpu.VMEM((N_HEADS, 2 * tq, 2 * V_DIM), F32)],
        compiler_params=_params(1),
        name="meta_attn",
    )(qh, kmeta, vmeta, biasm[:, :, :LANES], vaux[:LANES], lam_params, gsub)


def _prompt_attention(qh, kh, vh, kmeta, vmeta, c_log2, alibi, lam_params, gsub, *, tq, tk):
    t_q = qh.shape[1]
    biasf, biasm, vaux = alibi
    tables = _attn_tables(t_q // tq, tq, tk)
    q_spec = pl.BlockSpec((N_HEADS, tq, V_DIM), lambda s, qi, ki, kd, la: (0, qi[s], 0))
    kv_spec = pl.BlockSpec((N_HEADS, tk, V_DIM), lambda s, qi, ki, kd, la: (0, ki[s], 0))
    grid_spec = pltpu.PrefetchScalarGridSpec(
        num_scalar_prefetch=4,
        grid=(tables[0].shape[0],),
        in_specs=[
            pl.BlockSpec(memory_space=pltpu.SMEM),
            q_spec, kv_spec, kv_spec,
            _const_spec((N_HEADS, LANES, V_DIM)),
            _const_spec((N_HEADS, LANES, V_DIM)),
            _const_spec((N_HEADS, 1, LANES + tk)),
            _const_spec((N_HEADS, 1, tk)),
            _const_spec((LANES + tk, LANES)),
            _const_spec((4, HEAD_DIM)),
            _const_spec((1, V_DIM)),
        ],
        out_specs=pl.BlockSpec((tq, D_V), lambda s, qi, ki, kd, la: (qi[s], 0)),
        scratch_shapes=[
            pltpu.VMEM((N_HEADS, 2 * tq, 1), F32),
            pltpu.VMEM((N_HEADS, 2 * tq, 2 * V_DIM), F32),
            pltpu.VMEM((2 * tq, LANES + tk), F32),
            pltpu.VMEM((2 * tq, LANES + tk), F32),
        ],
    )
    return pl.pallas_call(
        functools.partial(_prompt_attn_kernel, tq=tq, tk=tk),
        grid_spec=grid_spec,
        out_shape=jax.ShapeDtypeStruct((t_q, D_V), BF16),
        compiler_params=_params(1),
        name="prompt_attn",
    )(*tables, c_log2, qh, kh, vh, kmeta, vmeta, biasf, biasm, vaux, lam_params, gsub)


def _sample_attn_kernel(*refs, pps, n_steps, past_len, n_q):
    wq_ref = refs[1]
    k_refs = refs[2:2 + pps]
    v_refs = refs[2 + pps:2 + 2 * pps]
    knew_ref, vnew_ref, bias_ref, c_ref, ones_ref, lam_ref, gsub_ref, o_ref, m_ref, acc_ref = refs[2 + 2 * pps:]
    j = pl.program_id(1)
    n_rows = N_HEADS * 2 * n_q
    page_rows = PAGE_SIZE * N_HEADS
    wq = wq_ref[...]
    c_rows = c_ref[...]

    def scores(keys, s_bias):
        return _nt_dot(wq, keys) + s_bias

    def softmax_pv(s, vals, c):
        m_old = m_ref[...]
        m_new = jnp.maximum(m_old, jnp.max(s, axis=-1, keepdims=True) + c)
        p = jnp.exp2(s - (m_new - c))
        vv = jnp.concatenate([vals, ones_ref[0:vals.shape[0], :]], axis=1)
        acc_ref[...] = (jnp.exp2(m_old - m_new) * acc_ref[...]
                        + jnp.dot(p.astype(BF16), vv, preferred_element_type=F32))
        m_ref[...] = m_new

    @pl.when(j == 0)
    def _init():
        m_ref[...] = jnp.full(m_ref.shape, NEG, F32)
        acc_ref[...] = jnp.zeros(acc_ref.shape, F32)

    as_rows = lambda r: r[...].reshape(page_rows, V_DIM).astype(BF16)
    c_step = c_rows * _int_to_f32(j * (pps * PAGE_SIZE) - past_len)
    def group_scores(g):
        lo = g * SAMPLE_PAGE_GROUP
        keys = jnp.concatenate([as_rows(r) for r in k_refs[lo:lo + SAMPLE_PAGE_GROUP]], axis=0)
        return scores(keys, bias_ref[:, lo * page_rows:(lo + SAMPLE_PAGE_GROUP) * page_rows])

    n_groups = pps // SAMPLE_PAGE_GROUP
    s_next = group_scores(0)
    for g in range(n_groups):
        s_cur = s_next
        if g + 1 < n_groups:
            s_next = group_scores(g + 1)
        lo = g * SAMPLE_PAGE_GROUP
        vals = jnp.concatenate([as_rows(r) for r in v_refs[lo:lo + SAMPLE_PAGE_GROUP]], axis=0)
        softmax_pv(s_cur, vals, c_step)

    @pl.when(j == n_steps - 1)
    def _finish():
        n_new = n_q * N_HEADS
        col = lax.broadcasted_iota(jnp.int32, (n_rows, n_new), 1)
        row = lax.broadcasted_iota(jnp.int32, (n_rows, n_new), 0)
        key = col >> _log2(N_HEADS)
        valid = ((col & (N_HEADS - 1)) == (row >> _log2(2 * n_q))) & (key <= (row & (n_q - 1)))
        s_bias = jnp.where(valid, c_rows * key.astype(F32), NEG)
        softmax_pv(scores(knew_ref[...].astype(BF16), s_bias), vnew_ref[...].astype(BF16), jnp.zeros((1, 1), F32))

        lam = _lambda_value(lam_ref)
        gsub = gsub_ref[...]
        acc = acc_ref[...]
        acc = acc[:, :V_DIM] * (1.0 / acc[:, V_DIM:V_DIM + 1])
        for h in range(N_HEADS):
            r0 = h * 2 * n_q
            o = acc[r0:r0 + n_q] - lam * acc[r0 + n_q:r0 + 2 * n_q]
            o_ref[:, h * V_DIM:(h + 1) * V_DIM] = _sub_layer_norm(o, gsub)


def _sample_attention(wq, cache_k, cache_v, page_table, knew, vnew, bias, c_rows, ones_col, lam_params, gsub, *,
                      pps):
    n_seq, n_rows, _ = wq.shape
    n_pages = page_table.shape[1]
    n_q = n_rows // (2 * N_HEADS)
    n_steps = n_pages // pps
    kern = functools.partial(_sample_attn_kernel, pps=pps, n_steps=n_steps,
                             past_len=n_pages * PAGE_SIZE, n_q=n_q)

    def page_spec(r):
        return pl.BlockSpec((None, PAGE_SIZE, N_HEADS, V_DIM),
                            lambda n, j, pt: (pt[n * n_pages + j * pps + r], 0, 0, 0))

    seq_spec = lambda rows, cols: pl.BlockSpec((None, rows, cols), lambda n, j, pt: (n, 0, 0))
    grid_spec = pltpu.PrefetchScalarGridSpec(
        num_scalar_prefetch=1,
        grid=(n_seq, n_steps),
        in_specs=([seq_spec(n_rows, V_DIM)]
                  + [page_spec(r) for r in range(pps)]
                  + [page_spec(r) for r in range(pps)]
                  + [seq_spec(n_q * N_HEADS, V_DIM), seq_spec(n_q * N_HEADS, V_DIM),
                     _const_spec(bias.shape), _const_spec((n_rows, 1)), _const_spec(ones_col.shape),
                     _const_spec((4, HEAD_DIM)), _const_spec((1, V_DIM))]),
        out_specs=seq_spec(n_q, D_V),
        scratch_shapes=[
            pltpu.VMEM((n_rows, 1), F32),
            pltpu.VMEM((n_rows, 2 * V_DIM), F32),
        ],
    )
    return pl.pallas_call(
        kern,
        grid_spec=grid_spec,
        out_shape=jax.ShapeDtypeStruct((n_seq, n_q, D_V), F32),
        compiler_params=_params(2),
        name="sample_attn",
    )(page_table.reshape(-1), wq, *([cache_k] * pps), *([cache_v] * pps),
      knew, vnew, bias, c_rows, ones_col, lam_params, gsub)


def kernel(x_prompt, x_sample, cache_k, cache_v, state_conv, state_ffn, page_table, meta_tokens, g_mix, g_ffn,
           w_pw1, w_dw, b_dw, ln_g, ln_b, w_pw2, g_kv, w_kv, g_k, w_q, g_q, lambda_params, g_subln, w_o,
           w_ffn_in, w_ffn_dw, b_ffn_dw, w_ffn_down):
    n_p, seq, _ = x_prompt.shape
    n_s, dec_seq, _ = x_sample.shape
    assert n_p == 1 and seq % ROW_TILE == 0 and seq % ATTN_TK == 0 and dec_seq == SUBLANES
    assert cache_k.shape[1:] == (PAGE_SIZE, N_HEADS, 2 * HEAD_DIM) and cache_v.shape[1:] == (PAGE_SIZE, N_HEADS, V_DIM)

    row = lambda a: a.reshape(1, -1).astype(F32)
    bf = lambda a: a.astype(BF16)
    wpw1, wpw2, wkv, wq, wo = bf(w_pw1[0]), bf(w_pw2[0]), bf(w_kv), bf(w_q[0]), bf(w_o[0])
    win, wdown = bf(w_ffn_in), bf(w_ffn_down)
    gk_t = jnp.tile(g_k.reshape(-1), N_HEADS).reshape(1, D_QK)
    gq_t = jnp.tile(g_q[0].reshape(-1), N_HEADS).reshape(1, D_QK)
    grp = np.arange(MXU_DIM) // HEAD_DIM
    ones_blk = jnp.asarray(grp[:, None] == grp[None, :], BF16)
    slopes = jnp.exp2(-8.0 * jnp.arange(1, N_HEADS + 1, dtype=F32) / N_HEADS)
    c_log2 = slopes * LOG2_E
    alibi = _alibi_tables(c_log2, ATTN_TK)
    lam_p = lambda_params[0].astype(F32)
    gsub = row(g_subln[0])

    def layer0(h, conv_past, ffn_past, nb, tm, tm_ffn):
        h1, conv_state = _conv_module(h, conv_past, row(g_mix[0]), wpw1, w_dw[0], row(b_dw[0]),
                                      row(ln_g[0]), row(ln_b[0]), wpw2, nb=nb, tm=tm)
        h2, ffn_state = _ffn(h1, ffn_past, row(g_ffn[0]), win[0], w_ffn_dw[0], row(b_ffn_dw[0]), wdown[0],
                             nb=nb, tm=tm_ffn)
        return h2, conv_state, ffn_state

    def project(h2d, tm, lead=None):
        return _kvq(h2d, row(g_kv), wkv, gk_t, row(g_mix[1]), wq, gq_t, ones_blk, tm=tm, lead=lead)

    def layer1_ffn(h, o, ffn_past, nb, tm):
        return _ffn(h, ffn_past, row(g_ffn[1]), win[1], w_ffn_dw[1], row(b_ffn_dw[1]), wdown[1],
                    nb=nb, tm=tm, o=o, wo=wo)

    h_m = meta_tokens.astype(F32)[None]
    zeros_conv = jnp.zeros((1, CONV_HALO, D_CONV), F32)
    zeros_ffn = jnp.zeros((1, FFN_HALO, D_FF), F32)
    h2_m, conv_m, ffn0_m = layer0(h_m, zeros_conv, zeros_ffn, 1, N_META, N_META)
    k_m, v_m, kh_m, vh_m, qh_m = project(h2_m[0], N_META)
    pad_keys = lambda a: jnp.pad(a, ((0, 0), (0, LANES - N_META), (0, 0)))
    kh_m, vh_m = pad_keys(kh_m), pad_keys(vh_m)
    o_m = _meta_attention(qh_m, kh_m, vh_m, alibi, lam_p, gsub)
    _, ffn1_m = layer1_ffn(h2_m, o_m[None], zeros_ffn, 1, N_META)

    h2_p, conv_p, ffn0_p = layer0(x_prompt, conv_m, ffn0_m, 1, ROW_TILE, FFN_ROW_TILE)
    k_p, v_p, kh_p, vh_p, qh_p = project(h2_p[0], ROW_TILE, lead=(k_m, v_m))
    o_p = _prompt_attention(qh_p, kh_p, vh_p, kh_m, vh_m, c_log2, alibi, lam_p, gsub, tq=ATTN_TQ, tk=ATTN_TK)
    y_p, ffn1_p = layer1_ffn(h2_p, o_p[None], ffn1_m, 1, FFN_ROW_TILE)

    h2_s, conv_s, ffn0_s = layer0(x_sample, state_conv[0], state_ffn[0], n_s, dec_seq, dec_seq)
    k_s, v_s, _, _, qh_s = project(h2_s.reshape(n_s * dec_seq, D_MODEL), n_s * dec_seq)
    n_rows = 2 * N_HEADS * dec_seq
    q_g = qh_s.reshape(N_HEADS, n_s, 1, dec_seq, V_DIM).transpose(1, 0, 2, 3, 4)
    half = (np.arange(V_DIM) // HEAD_DIM)[None, :] == np.arange(2)[:, None]
    wq_rows = jnp.where(jnp.asarray(half)[None, None, :, None, :], q_g, jnp.zeros((), BF16))
    wq_rows = wq_rows.reshape(n_s, n_rows, V_DIM)
    c_rows = jnp.repeat(c_log2, 2 * dec_seq).reshape(n_rows, 1)
    chunk_cols = PAGES_PER_STEP * PAGE_SIZE * N_HEADS
    col = np.arange(chunk_cols)
    head_match = jnp.asarray((col % N_HEADS)[None, :] == (np.arange(n_rows) // (2 * dec_seq))[:, None])
    bias = jnp.where(head_match, c_rows * jnp.asarray(col // N_HEADS, F32)[None, :], NEG)
    ones_np = np.zeros((SAMPLE_PAGE_GROUP * PAGE_SIZE * N_HEADS, LANES), np.float32)
    ones_np[:, 0] = 1.0
    ones_col = jnp.asarray(ones_np, BF16)
    new_rows = lambda a: a.reshape(n_s, dec_seq * N_HEADS, V_DIM)
    o_s = _sample_attention(wq_rows, cache_k, cache_v, page_table, new_rows(k_s), new_rows(v_s),
                            bias, c_rows, ones_col, lam_p, gsub, pps=PAGES_PER_STEP)
    y_s, ffn1_s = layer1_ffn(h2_s, o_s, state_ffn[1], n_s, dec_seq)

    t_all = N_META + seq
    k_prompt = k_p.reshape(1, t_all, N_HEADS, 2 * HEAD_DIM)
    v_prompt = v_p.reshape(1, t_all, N_HEADS, V_DIM)
    return (y_p, y_s, k_prompt, v_prompt,
            k_s.reshape(n_s, dec_seq, N_HEADS, 2 * HEAD_DIM), v_s.reshape(n_s, dec_seq, N_HEADS, V_DIM),
            conv_p[None], conv_s[None],
            jnp.stack([ffn0_p, ffn1_p]), jnp.stack([ffn0_s, ffn1_s]))
```
